```python
import math
import jax, jax.numpy as jnp
from jax import lax
import numpy as np

D_MODEL = 2048
BATCH = 4
SEQ = 2048
DEPTH = 4

N_EVEN_LAYERS = (DEPTH + 1) // 2
N_ODD_LAYERS = DEPTH // 2
RMS_EPS = 1e-6

SB_HEAD_DIM = 128
SB_HEADS = D_MODEL // 256
SB_WIDTH = SB_HEADS * SB_HEAD_DIM
SB_BLOCK = 128

GLA_HEADS = 4
GLA_HEAD_V = D_MODEL // (2 * GLA_HEADS)
GLA_HEAD_K = GLA_HEAD_V // 2
GLA_K_WIDTH = GLA_HEADS * GLA_HEAD_K
GLA_V_WIDTH = GLA_HEADS * GLA_HEAD_V
GLA_GATE_RANK = 16
GLA_GATE_NORMALIZER = 16.0
GLA_CHUNK = 64

AB_SPLIT = (SB_WIDTH, SB_WIDTH, SB_WIDTH, GLA_K_WIDTH, GLA_K_WIDTH, GLA_V_WIDTH, GLA_V_WIDTH, GLA_GATE_RANK)
AB_IN_COLS = sum(AB_SPLIT)
AB_OUT_WIDTH = SB_WIDTH + GLA_V_WIDTH

GDN_HEAD_DIM = 128
GDN_K_HEADS = D_MODEL // 128
GDN_V_HEADS = 2 * GDN_K_HEADS
GDN_K_WIDTH = GDN_K_HEADS * GDN_HEAD_DIM
GDN_V_WIDTH = GDN_V_HEADS * GDN_HEAD_DIM
GDN_CONV_K = 4
GDN_CONV_DIM = 2 * GDN_K_WIDTH + GDN_V_WIDTH
GDN_SPLIT = (GDN_CONV_DIM, GDN_V_WIDTH, GDN_V_HEADS, GDN_V_HEADS)
GDN_IN_COLS = sum(GDN_SPLIT)
GDN_CHUNK = 64
DT_MIN = 1e-3
DT_MAX = 1e-1

MEM_TOKENS = 256
MEM_HEADS = 4
MEM_HEAD_DIM = D_MODEL // MEM_HEADS

FFN_HIDDEN = -(-8 * D_MODEL // (3 * 256)) * 256

kernel_name = 'hybrid_stickbreak_gla_gdeltanet_memxattn'


def _split(x, sizes):
    offsets = [int(o) for o in np.cumsum(sizes)[:-1]]
    return jnp.split(x, offsets, axis=-1)


def rms_norm(x, g):
    xf = x.astype(jnp.float32)
    y = xf * lax.rsqrt(jnp.mean(xf * xf, axis=-1, keepdims=True) + RMS_EPS)
    return (y * g.astype(jnp.float32)).astype(x.dtype)


def l2_normalize(x):
    return x * lax.rsqrt(jnp.sum(x * x, axis=-1, keepdims=True) + RMS_EPS)


def stick_breaking_attention(q, k, v):
    T, d = q.shape[2], q.shape[3]
    scale = d ** -0.5
    outs = []
    for blk in range(T // SB_BLOCK):
        start = blk * SB_BLOCK
        stop = start + SB_BLOCK
        z = jnp.einsum('bhqd,bhkd->bhqk', q[:, :, start:stop], k[:, :, :stop]).astype(jnp.float32) * scale
        valid = jnp.arange(stop)[None, :] < (start + jnp.arange(SB_BLOCK))[:, None]
        log_keep = jnp.where(valid, jax.nn.log_sigmoid(-z), 0.0)
        log_after = lax.cumsum(log_keep, axis=3, reverse=True) - log_keep
        w = jnp.where(valid, jnp.exp(jax.nn.log_sigmoid(z) + log_after), 0.0)
        outs.append(jnp.einsum('bhqk,bhkd->bhqd', w.astype(v.dtype), v[:, :, :stop]))
    return jnp.concatenate(outs, axis=2)


def gla_chunked(q, k, v, log_a):
    B, H, T, dk = q.shape
    dv = v.shape[-1]
    C = GLA_CHUNK
    N = T // C
    q = (q * dk ** -0.5).reshape(B, H, N, C, dk)
    k = k.reshape(B, H, N, C, dk)
    v = v.reshape(B, H, N, C, dv)
    G = jnp.cumsum(log_a.reshape(B, H, N, C, dk), axis=3)
    G_last = G[:, :, :, -1:]
    q_dec = q * jnp.exp(G)
    scores = jnp.einsum('bhncd,bhnsd->bhncs', q_dec, k * jnp.exp(-G))
    causal = jnp.tril(jnp.ones((C, C), dtype=bool))
    o_intra = jnp.einsum('bhncs,bhnsv->bhncv', jnp.where(causal, scores, 0.0), v)
    dS = jnp.einsum('bhncd,bhncv->bhndv', k * jnp.exp(G_last - G), v)
    chunk_decay = jnp.exp(G_last[:, :, :, 0])

    def step(S, inp):
        dec, ds = inp
        return S * dec[..., None] + ds, S

    S0 = jnp.zeros((B, H, dk, dv), q.dtype)
    _, S_prev = lax.scan(step, S0, (jnp.moveaxis(chunk_decay, 2, 0), jnp.moveaxis(dS, 2, 0)))
    o_inter = jnp.einsum('bhncd,nbhdv->bhncv', q_dec, S_prev)
    return (o_intra + o_inter).reshape(B, H, T, dv)


def gated_delta_rule_chunked(q, k, v, beta, g):
    B, H, T, dk = q.shape
    dv = v.shape[-1]
    C = GDN_CHUNK
    N = T // C
    q = (q * dk ** -0.5).reshape(B, H, N, C, dk)
    k = k.reshape(B, H, N, C, dk)
    v = v.reshape(B, H, N, C, dv)
    beta = beta.reshape(B, H, N, C, 1)
    g = jnp.cumsum(g.reshape(B, H, N, C), axis=3)
    incl = jnp.tril(jnp.ones((C, C), dtype=bool))
    strict = jnp.tril(jnp.ones((C, C), dtype=bool), -1)
    diff = g[..., :, None] - g[..., None, :]
    decay = jnp.where(incl, jnp.exp(jnp.where(incl, diff, 0.0)), 0.0)
    kb = k * beta
    L = jnp.where(strict, jnp.einsum('bhncd,bhnsd->bhncs', kb, k) * decay, 0.0)
    eye = jnp.eye(C, dtype=q.dtype)
    rhs = jnp.concatenate([v * beta, kb * jnp.exp(g)[..., None]], axis=-1)
    sol = lax.linalg.triangular_solve(L + eye, rhs, left_side=True, lower=True, unit_diagonal=True)
    u, w = sol[..., :dv], sol[..., dv:]
    qk = jnp.einsum('bhncd,bhnsd->bhncs', q, k) * decay
    q_dec = q * jnp.exp(g)[..., None]
    g_last = g[..., -1]
    k_dec = k * jnp.exp(g_last[..., None] - g)[..., None]

    def step(S, inp):
        u_n, w_n, qk_n, qd_n, kd_n, gl_n = inp
        v_new = u_n - jnp.einsum('bhcd,bhdv->bhcv', w_n, S)
        o = jnp.einsum('bhcd,bhdv->bhcv', qd_n, S) + jnp.einsum('bhcs,bhsv->bhcv', qk_n, v_new)
        S = S * jnp.exp(gl_n)[..., None, None] + jnp.einsum('bhcd,bhcv->bhdv', kd_n, v_new)
        return S, o

    xs = (jnp.moveaxis(u, 2, 0), jnp.moveaxis(w, 2, 0), jnp.moveaxis(qk, 2, 0),
          jnp.moveaxis(q_dec, 2, 0), jnp.moveaxis(k_dec, 2, 0), jnp.moveaxis(g_last, 2, 0))
    _, o = lax.scan(step, jnp.zeros((B, H, dk, dv), q.dtype), xs)
    return jnp.moveaxis(o, 0, 2).reshape(B, H, T, dv)


def causal_depthwise_conv(x, w):
    K, C = w.shape
    return lax.conv_general_dilated(x, w[:, None, :], window_strides=(1,), padding=[(K - 1, 0)],
                                    dimension_numbers=('NWC', 'WIO', 'NWC'), feature_group_count=C)


def even_mixer(h, w_in, gate_w2, gate_b, gla_norm_g, w_out):
    B, T, _ = h.shape
    sb_q, sb_k, sb_v, gq, gk, gv, gr, g_lr = _split(h @ w_in, AB_SPLIT)

    def heads(t, n):
        return t.reshape(B, T, n, -1).transpose(0, 2, 1, 3)

    o_sb = stick_breaking_attention(heads(sb_q, SB_HEADS), heads(sb_k, SB_HEADS), heads(sb_v, SB_HEADS))
    o_sb = o_sb.transpose(0, 2, 1, 3).reshape(B, T, SB_WIDTH)

    f32 = jnp.float32
    log_a = jax.nn.log_sigmoid((g_lr @ gate_w2 + gate_b).astype(f32)) / GLA_GATE_NORMALIZER
    o_gla = gla_chunked(heads(gq, GLA_HEADS).astype(f32), heads(gk, GLA_HEADS).astype(f32),
                        heads(gv, GLA_HEADS).astype(f32), heads(log_a, GLA_HEADS))
    o_gla = o_gla.transpose(0, 2, 1, 3)
    o_gla = rms_norm(o_gla, gla_norm_g) * jax.nn.silu(gr.reshape(B, T, GLA_HEADS, GLA_HEAD_V).astype(f32))
    o = jnp.concatenate([o_sb, o_gla.reshape(B, T, GLA_V_WIDTH).astype(h.dtype)], axis=-1)
    return o @ w_out


def odd_mixer(h, w_in, conv_w, a_log, dt_bias, norm_g, w_out):
    B, T, _ = h.shape
    f32 = jnp.float32
    qkv, z, b, a = _split(h @ w_in, GDN_SPLIT)
    qkv = jax.nn.silu(causal_depthwise_conv(qkv, conv_w))
    q, k, v = _split(qkv, (GDN_K_WIDTH, GDN_K_WIDTH, GDN_V_WIDTH))
    rep = GDN_V_HEADS // GDN_K_HEADS
    q = jnp.repeat(l2_normalize(q.reshape(B, T, GDN_K_HEADS, GDN_HEAD_DIM).astype(f32)), rep, axis=2)
    k = jnp.repeat(l2_normalize(k.reshape(B, T, GDN_K_HEADS, GDN_HEAD_DIM).astype(f32)), rep, axis=2)
    v = v.reshape(B, T, GDN_V_HEADS, GDN_HEAD_DIM).astype(f32)
    beta = jax.nn.sigmoid(b.astype(f32))
    g = -jnp.exp(a_log.astype(f32)) * jax.nn.softplus(a.astype(f32) + dt_bias.astype(f32))
    o = gated_delta_rule_chunked(q.transpose(0, 2, 1, 3), k.transpose(0, 2, 1, 3), v.transpose(0, 2, 1, 3),
                                 beta.transpose(0, 2, 1), g.transpose(0, 2, 1))
    o = o.transpose(0, 2, 1, 3)
    o = rms_norm(o, norm_g) * jax.nn.silu(z.reshape(B, T, GDN_V_HEADS, GDN_HEAD_DIM).astype(f32))
    return o.reshape(B, T, GDN_V_WIDTH).astype(h.dtype) @ w_out


def memory_cross_attention(h, mem_k, mem_v, w_q, w_o):
    B, T, _ = h.shape
    q = (h @ w_q).reshape(B, T, MEM_HEADS, MEM_HEAD_DIM)
    s = jnp.einsum('bthd,bmhd->bhtm', q, mem_k).astype(jnp.float32) * MEM_HEAD_DIM ** -0.5
    p = jax.nn.softmax(s, axis=-1).astype(h.dtype)
    o = jnp.einsum('bhtm,bmhd->bthd', p, mem_v).reshape(B, T, D_MODEL)
    return o @ w_o


def swiglu(h, w_gate_up, w_down):
    gate, up = jnp.split(h @ w_gate_up, 2, axis=-1)
    return (jax.nn.silu(gate) * up) @ w_down


def _dense(key, shape, fan_in):
    return jax.random.normal(key, shape, jnp.float32) * fan_in ** -0.5


def _gain(key, shape):
    return 1.0 + 0.02 * jax.random.normal(key, shape, jnp.float32)


def setup_inputs(seed: int = 0) -> dict:
    key = jax.random.key(seed)
    ks = jax.random.split(key, 25)
    ne, no = N_EVEN_LAYERS, N_ODD_LAYERS
    dt = jnp.exp(jax.random.uniform(ks[14], (no, GDN_V_HEADS), jnp.float32, math.log(DT_MIN), math.log(DT_MAX)))
    return {
        'x': jax.random.normal(ks[0], (BATCH, SEQ, D_MODEL), jnp.float32),
        'mem': jax.random.normal(ks[1], (BATCH, MEM_TOKENS, D_MODEL), jnp.float32),
        'mem_norm_g': _gain(ks[2], (D_MODEL,)),
        'mem_w_kv': _dense(ks[3], (D_MODEL, 2 * D_MODEL), D_MODEL),
        'mix_pre_g': _gain(ks[4], (DEPTH, D_MODEL)),
        'mix_post_g': _gain(ks[5], (DEPTH, D_MODEL)),
        'ab_w_in': _dense(ks[6], (ne, D_MODEL, AB_IN_COLS), D_MODEL),
        'gla_gate_w2': _dense(ks[7], (ne, GLA_GATE_RANK, GLA_K_WIDTH), GLA_GATE_RANK),
        'gla_gate_b': 0.1 * jax.random.normal(ks[8], (ne, GLA_K_WIDTH), jnp.float32),
        'gla_norm_g': _gain(ks[9], (ne, GLA_HEAD_V)),
        'ab_w_out': _dense(ks[10], (ne, AB_OUT_WIDTH, D_MODEL), AB_OUT_WIDTH),
        'gdn_w_in': _dense(ks[11], (no, D_MODEL, GDN_IN_COLS), D_MODEL),
        'gdn_conv_w': _dense(ks[12], (no, GDN_CONV_K, GDN_CONV_DIM), GDN_CONV_K),
        'gdn_a_log': jnp.log(jax.random.uniform(ks[13], (no, GDN_V_HEADS), jnp.float32, 1.0, 16.0)),
        'gdn_dt_bias': dt + jnp.log(-jnp.expm1(-dt)),
        'gdn_norm_g': _gain(ks[15], (no, GDN_HEAD_DIM)),
        'gdn_w_out': _dense(ks[16], (no, GDN_V_WIDTH, D_MODEL), GDN_V_WIDTH),
        'xattn_pre_g': _gain(ks[17], (DEPTH, D_MODEL)),
        'xattn_post_g': _gain(ks[18], (DEPTH, D_MODEL)),
        'xattn_w_q': _dense(ks[19], (DEPTH, D_MODEL, D_MODEL), D_MODEL),
        'xattn_w_o': _dense(ks[20], (DEPTH, D_MODEL, D_MODEL), D_MODEL),
        'ffn_pre_g': _gain(ks[21], (DEPTH, D_MODEL)),
        'ffn_post_g': _gain(ks[22], (DEPTH, D_MODEL)),
        'ffn_w_gate_up': _dense(ks[23], (DEPTH, D_MODEL, 2 * FFN_HIDDEN), D_MODEL),
        'ffn_w_down': _dense(ks[24], (DEPTH, FFN_HIDDEN, D_MODEL), FFN_HIDDEN),
    }


def reference(x, mem, mem_norm_g, mem_w_kv, mix_pre_g, mix_post_g, ab_w_in, gla_gate_w2, gla_gate_b,
              gla_norm_g, ab_w_out, gdn_w_in, gdn_conv_w, gdn_a_log, gdn_dt_bias, gdn_norm_g, gdn_w_out,
              xattn_pre_g, xattn_post_g, xattn_w_q, xattn_w_o, ffn_pre_g, ffn_post_g, ffn_w_gate_up, ffn_w_down):
    B, M, _ = mem.shape
    kv = (rms_norm(mem, mem_norm_g) @ mem_w_kv).reshape(B, M, 2, MEM_HEADS, MEM_HEAD_DIM)
    mem_k, mem_v = kv[:, :, 0], kv[:, :, 1]

    h = x
    for layer in range(DEPTH):
        j = layer // 2
        u = rms_norm(h, mix_pre_g[layer])
        if layer % 2 == 0:
            u = even_mixer(u, ab_w_in[j], gla_gate_w2[j], gla_gate_b[j], gla_norm_g[j], ab_w_out[j])
        else:
            u = odd_mixer(u, gdn_w_in[j], gdn_conv_w[j], gdn_a_log[j], gdn_dt_bias[j], gdn_norm_g[j], gdn_w_out[j])
        h = h + rms_norm(u, mix_post_g[layer])
        u = memory_cross_attention(rms_norm(h, xattn_pre_g[layer]), mem_k, mem_v, xattn_w_q[layer], xattn_w_o[layer])
        h = h + rms_norm(u, xattn_post_g[layer])
        u = swiglu(rms_norm(h, ffn_pre_g[layer]), ffn_w_gate_up[layer], ffn_w_down[layer])
        h = h + rms_norm(u, ffn_post_g[layer])
    return h
```

```python
import functools

import jax
import jax.numpy as jnp
from jax import lax
from jax.experimental import pallas as pl
from jax.experimental.pallas import tpu as pltpu

F32 = jnp.float32
BF16 = jnp.bfloat16
RMS_EPS = 1e-6

SB_HEADS = 8
SB_DIM = 128
GLA_HEADS = 4
GLA_DK = 128
GLA_DV = 256
GLA_RANK = 16
GLA_NORMALIZER = 16.0
GDN_KH = 16
GDN_VH = 32
GDN_DIM = 128
GDN_CONV_K = 4
MEM_HEADS = 4
CHUNK = 64
LANES = 128
ROW_CHUNK = 64
SB_LOG_CUTOFF = -104.0


def _cparams(semantics, vmem_mib):
    return pltpu.CompilerParams(dimension_semantics=semantics, vmem_limit_bytes=vmem_mib << 20)


def _rms_rows(x, g):
    ms = jnp.mean(x * x, axis=-1, keepdims=True)
    return x * lax.rsqrt(ms + RMS_EPS) * g


def _softplus(z):
    return jnp.maximum(z, 0.0) + jnp.log1p(jnp.exp(-jnp.abs(z)))


def _silu(x):
    return x * jax.nn.sigmoid(x)


def _split3(x):
    hi = x.astype(BF16)
    r1 = x - hi.astype(F32)
    mid = r1.astype(BF16)
    lo = (r1 - mid.astype(F32)).astype(BF16)
    return hi, mid, lo


def _for_rows(n_rows, body):
    def step(r, c):
        body(pl.ds(pl.multiple_of(r * ROW_CHUNK, ROW_CHUNK), ROW_CHUNK))
        return c
    lax.fori_loop(0, n_rows // ROW_CHUNK, step, 0)


def _prenorm_to(xn_ref, h_ref, g_ref):
    def body(rows):
        xn_ref[rows, :] = _rms_rows(h_ref[rows, :], g_ref[...]).astype(xn_ref.dtype)
    _for_rows(h_ref.shape[0], body)


def _postnorm_residual_to(o_ref, h_ref, acc_ref, g_ref):
    def body(rows):
        o_ref[rows, :] = h_ref[rows, :] + _rms_rows(acc_ref[rows, :], g_ref[...])
    _for_rows(h_ref.shape[0], body)


def _norm_matmul_kernel(h_ref, g_ref, w_ref, o_ref, xn_ref):
    @pl.when(pl.program_id(1) == 0)
    def _():
        _prenorm_to(xn_ref, h_ref, g_ref)
    o_ref[...] = jnp.dot(xn_ref[...], w_ref[...], preferred_element_type=F32).astype(o_ref.dtype)


def norm_matmul(h, g, w, *, tm, tn, out_dtype):
    m, d = h.shape
    n = w.shape[1]
    return pl.pallas_call(
        _norm_matmul_kernel,
        grid=(m // tm, n // tn),
        in_specs=[pl.BlockSpec((tm, d), lambda i, j: (i, 0)),
                  pl.BlockSpec((1, d), lambda i, j: (0, 0)),
                  pl.BlockSpec((d, tn), lambda i, j: (0, j))],
        out_specs=pl.BlockSpec((tm, tn), lambda i, j: (i, j)),
        out_shape=jax.ShapeDtypeStruct((m, n), out_dtype),
        scratch_shapes=[pltpu.VMEM((tm, d), BF16)],
        compiler_params=_cparams(("parallel", "arbitrary"), 48),
        name="norm_matmul",
    )(h, g, w)


def _matmul_post_kernel(a_ref, w_ref, g_ref, h_ref, o_ref, acc_ref):
    acc_ref[...] = jnp.dot(a_ref[...], w_ref[...], preferred_element_type=F32)
    _postnorm_residual_to(o_ref, h_ref, acc_ref, g_ref)


def matmul_post(a, w, g, h, *, tm):
    m, k = a.shape
    d = w.shape[1]
    return pl.pallas_call(
        _matmul_post_kernel,
        grid=(m // tm,),
        in_specs=[pl.BlockSpec((tm, k), lambda i: (i, 0)),
                  pl.BlockSpec((k, d), lambda i: (0, 0), pipeline_mode=pl.Buffered(1)),
                  pl.BlockSpec((1, d), lambda i: (0, 0)),
                  pl.BlockSpec((tm, d), lambda i: (i, 0))],
        out_specs=pl.BlockSpec((tm, d), lambda i: (i, 0)),
        out_shape=jax.ShapeDtypeStruct((m, d), F32),
        scratch_shapes=[pltpu.VMEM((tm, d), F32)],
        compiler_params=_cparams(("parallel",), ((2 * tm * k * 2 + k * d * 2 + 5 * tm * d * 4) >> 20) + 8),
        name="matmul_post",
    )(a, w, g, h)


def _ffn_kernel(h_ref, gpre_ref, wg_ref, wu_ref, wd_ref, gpost_ref, o_ref, xn_ref, acc_ref):
    j = pl.program_id(1)

    @pl.when(j == 0)
    def _():
        _prenorm_to(xn_ref, h_ref, gpre_ref)

    xn = xn_ref[...]
    gate = jnp.dot(xn, wg_ref[...], preferred_element_type=F32)
    up = jnp.dot(xn, wu_ref[...], preferred_element_type=F32)
    act = (_silu(gate) * up).astype(BF16)
    part = jnp.dot(act, wd_ref[...], preferred_element_type=F32)

    @pl.when(j == 0)
    def _():
        acc_ref[...] = part

    @pl.when(j > 0)
    def _():
        acc_ref[...] += part

    @pl.when(j == pl.num_programs(1) - 1)
    def _():
        _postnorm_residual_to(o_ref, h_ref, acc_ref, gpost_ref)


def ffn_layer(h, gpre, w_gate_up, w_down, gpost, *, tm, tf):
    m, d = h.shape
    f = w_down.shape[0]
    nf = f // tf
    return pl.pallas_call(
        _ffn_kernel,
        grid=(m // tm, nf),
        in_specs=[pl.BlockSpec((tm, d), lambda i, j: (i, 0)),
                  pl.BlockSpec((1, d), lambda i, j: (0, 0)),
                  pl.BlockSpec((d, tf), lambda i, j: (0, j)),
                  pl.BlockSpec((d, tf), lambda i, j: (0, j + nf)),
                  pl.BlockSpec((tf, d), lambda i, j: (j, 0)),
                  pl.BlockSpec((1, d), lambda i, j: (0, 0))],
        out_specs=pl.BlockSpec((tm, d), lambda i, j: (i, 0)),
        out_shape=jax.ShapeDtypeStruct((m, d), F32),
        scratch_shapes=[pltpu.VMEM((tm, d), BF16), pltpu.VMEM((tm, d), F32)],
        compiler_params=_cparams(("parallel", "arbitrary"), 48),
        name="ffn_layer",
    )(h, gpre, w_gate_up, w_gate_up, w_down, gpost)


def _xattn_kernel(h_ref, gpre_ref, wq_ref, k_ref, v_ref, wo_ref, gpost_ref, o_ref, xn_ref, acc_ref):
    j = pl.program_id(1)

    @pl.when(j == 0)
    def _():
        _prenorm_to(xn_ref, h_ref, gpre_ref)

    hd = wq_ref.shape[1]
    q = jnp.dot(xn_ref[...], wq_ref[...], preferred_element_type=F32)
    s = lax.dot_general(q.astype(BF16), k_ref[...], (((1,), (1,)), ((), ())),
                        preferred_element_type=F32) * (hd ** -0.5)
    e = jnp.exp(s - jnp.max(s, axis=-1, keepdims=True))
    p = e / jnp.sum(e, axis=-1, keepdims=True)
    o = jnp.dot(p.astype(BF16), v_ref[...], preferred_element_type=F32)
    part = jnp.dot(o.astype(BF16), wo_ref[...], preferred_element_type=F32)

    @pl.when(j == 0)
    def _():
        acc_ref[...] = part

    @pl.when(j > 0)
    def _():
        acc_ref[...] += part

    @pl.when(j == pl.num_programs(1) - 1)
    def _():
        _postnorm_residual_to(o_ref, h_ref, acc_ref, gpost_ref)


def xattn_layer(h, gpre, wq, kv, wo, gpost, *, tm, seq):
    m, d = h.shape
    hd = d // MEM_HEADS
    mt = kv.shape[0] // (m // seq)
    per_b = seq // tm
    return pl.pallas_call(
        _xattn_kernel,
        grid=(m // tm, MEM_HEADS),
        in_specs=[pl.BlockSpec((tm, d), lambda i, j: (i, 0)),
                  pl.BlockSpec((1, d), lambda i, j: (0, 0)),
                  pl.BlockSpec((d, hd), lambda i, j: (0, j)),
                  pl.BlockSpec((mt, hd), lambda i, j: (i // per_b, j)),
                  pl.BlockSpec((mt, hd), lambda i, j: (i // per_b, j + MEM_HEADS)),
                  pl.BlockSpec((hd, d), lambda i, j: (j, 0)),
                  pl.BlockSpec((1, d), lambda i, j: (0, 0))],
        out_specs=pl.BlockSpec((tm, d), lambda i, j: (i, 0)),
        out_shape=jax.ShapeDtypeStruct((m, d), F32),
        scratch_shapes=[pltpu.VMEM((tm, d), BF16), pltpu.VMEM((tm, d), F32)],
        compiler_params=_cparams(("parallel", "arbitrary"), 48),
        name="xattn_layer",
    )(h, gpre, wq, kv, kv, wo, gpost)


def _sb_kernel(q_ref, k_ref, v_ref, o_ref, carry_ref, acc_ref):
    tq = q_ref.shape[0]
    qi = pl.program_id(2)
    scale = SB_DIM ** -0.5
    q = q_ref[...].astype(BF16)
    row = lax.broadcasted_iota(jnp.int32, (tq, tq), 0)
    col = lax.broadcasted_iota(jnp.int32, (tq, tq), 1)
    after = jnp.where(row > col, 1.0, 0.0).astype(BF16)

    def block(j, diagonal):
        ks = pl.ds(pl.multiple_of(j * tq, tq), tq)
        kb = k_ref[ks, :].astype(BF16)
        vb = v_ref[ks, :].astype(BF16)
        z = lax.dot_general(q, kb, (((1,), (1,)), ((), ())), preferred_element_type=F32) * scale
        sp = _softplus(z)
        log_keep = -sp
        if diagonal:
            valid = col < row
            log_keep = jnp.where(valid, log_keep, 0.0)
        hi, mid, lo = _split3(log_keep)
        later = (jnp.dot(hi, after, preferred_element_type=F32)
                 + jnp.dot(mid, after, preferred_element_type=F32)
                 + jnp.dot(lo, after, preferred_element_type=F32))
        w = jnp.exp(z - sp + later + carry_ref[...])
        if diagonal:
            w = jnp.where(valid, w, 0.0)
        acc_ref[...] += jnp.dot(w.astype(BF16), vb, preferred_element_type=F32)
        carry_ref[...] += jnp.sum(log_keep, axis=-1, keepdims=True)
        return jnp.max(carry_ref[...])

    carry_ref[...] = jnp.zeros_like(carry_ref)
    acc_ref[...] = jnp.zeros_like(acc_ref)
    top = block(qi, True)

    def cond(state):
        j, top = state
        return jnp.logical_and(j >= 0, top > SB_LOG_CUTOFF)

    def body(state):
        j, _ = state
        return j - 1, block(j, False)

    lax.while_loop(cond, body, (qi - 1, top))
    o_ref[...] = acc_ref[...].astype(o_ref.dtype)


def sb_attention(proj, *, batch, seq, tq):
    m = proj.shape[0]
    nq = seq // tq
    return pl.pallas_call(
        _sb_kernel,
        grid=(batch, SB_HEADS, nq),
        in_specs=[pl.BlockSpec((tq, SB_DIM), lambda b, h, i: (b * nq + i, h)),
                  pl.BlockSpec((seq, SB_DIM), lambda b, h, i: (b, SB_HEADS + h)),
                  pl.BlockSpec((seq, SB_DIM), lambda b, h, i: (b, 2 * SB_HEADS + h))],
        out_specs=pl.BlockSpec((tq, SB_DIM), lambda b, h, i: (b * nq + i, h)),
        out_shape=jax.ShapeDtypeStruct((m, SB_HEADS * SB_DIM), BF16),
        scratch_shapes=[pltpu.VMEM((tq, 1), F32), pltpu.VMEM((tq, SB_DIM), F32)],
        compiler_params=_cparams(("parallel", "parallel", "arbitrary"), 32),
        name="sb_attention",
    )(proj, proj, proj)


def _chunk_cumsum_matrix(n):
    row = lax.broadcasted_iota(jnp.int32, (n, n), 0)
    col = lax.broadcasted_iota(jnp.int32, (n, n), 1)
    same = (row // CHUNK) == (col // CHUNK)
    return jnp.where(jnp.logical_and(same, col <= row), 1.0, 0.0).astype(BF16)


def _chunk_cumsum(x):
    tri = _chunk_cumsum_matrix(x.shape[0])
    hi, mid, lo = _split3(x)
    return (jnp.dot(tri, hi, preferred_element_type=F32)
            + jnp.dot(tri, mid, preferred_element_type=F32)
            + jnp.dot(tri, lo, preferred_element_type=F32))


def _gla_gate_kernel(h_ref, gpre_ref, w1_ref, w2_ref, b_ref, o_ref):
    xn = _rms_rows(h_ref[...], gpre_ref[...]).astype(BF16)
    low = jnp.dot(xn, w1_ref[...], preferred_element_type=F32)
    pre = jnp.dot(low.astype(BF16), w2_ref[...], preferred_element_type=F32) + b_ref[...]
    log_a = -_softplus(-pre) * (1.0 / GLA_NORMALIZER)
    o_ref[...] = _chunk_cumsum(log_a)


def gla_gate(h, gpre, w1, w2, b, *, tm):
    m, d = h.shape
    n = w2.shape[1]
    return pl.pallas_call(
        _gla_gate_kernel,
        grid=(m // tm,),
        in_specs=[pl.BlockSpec((tm, d), lambda i: (i, 0)),
                  pl.BlockSpec((1, d), lambda i: (0, 0)),
                  pl.BlockSpec((d, LANES), lambda i: (0, 0)),
                  pl.BlockSpec((LANES, n), lambda i: (0, 0)),
                  pl.BlockSpec((1, n), lambda i: (0, 0))],
        out_specs=pl.BlockSpec((tm, n), lambda i: (i, 0)),
        out_shape=jax.ShapeDtypeStruct((m, n), F32),
        compiler_params=_cparams(("parallel",), 32),
        name="gla_gate",
    )(h, gpre, w1, w2, b)


def _gla_kernel(q_ref, k_ref, v_ref, r_ref, gc_ref, ng_ref, o_ref, st_ref):
    @pl.when(pl.program_id(2) == 0)
    def _():
        st_ref[...] = jnp.zeros_like(st_ref)

    tb = q_ref.shape[0]
    row = lax.broadcasted_iota(jnp.int32, (CHUNK, CHUNK), 0)
    col = lax.broadcasted_iota(jnp.int32, (CHUNK, CHUNK), 1)
    causal = col <= row
    nt = (((1,), (1,)), ((), ()))
    tn = (((0,), (0,)), ((), ()))
    for c in range(tb // CHUNK):
        rows = slice(c * CHUNK, (c + 1) * CHUNK)
        g = gc_ref[rows, :]
        g_last = g[CHUNK - 1:CHUNK, :]
        k = k_ref[rows, :]
        v = v_ref[rows, :].astype(BF16)
        q_dec = (q_ref[rows, :] * (GLA_DK ** -0.5) * jnp.exp(g)).astype(BF16)
        k_inv = (k * jnp.exp(-g)).astype(BF16)
        k_dec = (k * jnp.exp(g_last - g)).astype(BF16)
        scores = lax.dot_general(q_dec, k_inv, nt, preferred_element_type=F32)
        scores = jnp.where(causal, scores, 0.0).astype(BF16)
        st = st_ref[...]
        o = (jnp.dot(scores, v, preferred_element_type=F32)
             + lax.dot_general(q_dec, st.astype(BF16), nt, preferred_element_type=F32))
        st_ref[...] = st * jnp.exp(g_last) + lax.dot_general(v, k_dec, tn, preferred_element_type=F32)
        o_ref[rows, :] = (_rms_rows(o, ng_ref[...]) * _silu(r_ref[rows, :])).astype(o_ref.dtype)


def gla_attention(proj, gc, norm_g, *, batch, seq, tb):
    m = proj.shape[0]
    nt = seq // tb
    qk0 = 3 * SB_HEADS * SB_DIM // GLA_DK
    v0 = (3 * SB_HEADS * SB_DIM + 2 * GLA_HEADS * GLA_DK) // GLA_DV
    return pl.pallas_call(
        _gla_kernel,
        grid=(batch, GLA_HEADS, nt),
        in_specs=[pl.BlockSpec((tb, GLA_DK), lambda b, h, t: (b * nt + t, qk0 + h)),
                  pl.BlockSpec((tb, GLA_DK), lambda b, h, t: (b * nt + t, qk0 + GLA_HEADS + h)),
                  pl.BlockSpec((tb, GLA_DV), lambda b, h, t: (b * nt + t, v0 + h)),
                  pl.BlockSpec((tb, GLA_DV), lambda b, h, t: (b * nt + t, v0 + GLA_HEADS + h)),
                  pl.BlockSpec((tb, GLA_DK), lambda b, h, t: (b * nt + t, h)),
                  pl.BlockSpec((1, GLA_DV), lambda b, h, t: (0, 0))],
        out_specs=pl.BlockSpec((tb, GLA_DV), lambda b, h, t: (b * nt + t, h)),
        out_shape=jax.ShapeDtypeStruct((m, GLA_HEADS * GLA_DV), BF16),
        scratch_shapes=[pltpu.VMEM((GLA_DV, GLA_DK), F32)],
        compiler_params=_cparams(("parallel", "parallel", "arbitrary"), 32),
        name="gla_attention",
    )(proj, proj, proj, proj, gc, norm_g)


def _gdn_gate_kernel(h_ref, gpre_ref, w_ref, alog_ref, dtb_ref, o_ref):
    xn = _rms_rows(h_ref[...], gpre_ref[...]).astype(BF16)
    ba = jnp.dot(xn, w_ref[...], preferred_element_type=F32)
    beta = jax.nn.sigmoid(ba)
    g = -jnp.exp(alog_ref[...]) * _softplus(ba + dtb_ref[...])
    lane = lax.broadcasted_iota(jnp.int32, ba.shape, 1)
    o_ref[...] = jnp.where(lane < GDN_VH, beta, _chunk_cumsum(g))


def gdn_gate(h, gpre, w, alog, dtb, *, tm):
    m, d = h.shape
    return pl.pallas_call(
        _gdn_gate_kernel,
        grid=(m // tm,),
        in_specs=[pl.BlockSpec((tm, d), lambda i: (i, 0)),
                  pl.BlockSpec((1, d), lambda i: (0, 0)),
                  pl.BlockSpec((d, LANES), lambda i: (0, 0)),
                  pl.BlockSpec((1, LANES), lambda i: (0, 0)),
                  pl.BlockSpec((1, LANES), lambda i: (0, 0))],
        out_specs=pl.BlockSpec((tm, LANES), lambda i: (i, 0)),
        out_shape=jax.ShapeDtypeStruct((m, LANES), F32),
        compiler_params=_cparams(("parallel",), 32),
        name="gdn_gate",
    )(h, gpre, w, alog, dtb)


def _gdn_conv_kernel(x_ref, prev_ref, w_ref, o_ref, xs_ref, *, n_norm_tiles):
    tb, tc = x_ref.shape
    halo = 8
    first = pl.program_id(1) == 0
    xs_ref[0:halo, :] = jnp.where(first, 0.0, prev_ref[...])
    xs_ref[halo:halo + tb, :] = x_ref[...]
    y = jnp.zeros((tb, tc), F32)
    for i in range(GDN_CONV_K):
        off = halo - (GDN_CONV_K - 1) + i
        y = y + w_ref[i:i + 1, :] * xs_ref[off:off + tb, :]
    y = _silu(y)
    is_qk = pl.program_id(2) < n_norm_tiles

    @pl.when(is_qk)
    def _():
        for hh in range(tc // GDN_DIM):
            cols = slice(hh * GDN_DIM, (hh + 1) * GDN_DIM)
            yh = y[:, cols]
            ss = jnp.sum(yh * yh, axis=-1, keepdims=True)
            o_ref[:, cols] = (yh * lax.rsqrt(ss + RMS_EPS)).astype(o_ref.dtype)

    @pl.when(jnp.logical_not(is_qk))
    def _():
        o_ref[...] = y.astype(o_ref.dtype)


def gdn_conv(proj, conv_w, *, batch, seq, tb, tc, out_dtype):
    m = proj.shape[0]
    conv_dim = conv_w.shape[1]
    nt = seq // tb
    halo_blocks = tb // 8
    return pl.pallas_call(
        functools.partial(_gdn_conv_kernel, n_norm_tiles=2 * GDN_KH * GDN_DIM // tc),
        grid=(batch, nt, conv_dim // tc),
        in_specs=[pl.BlockSpec((tb, tc), lambda b, t, c: (b * nt + t, c)),
                  pl.BlockSpec((8, tc), lambda b, t, c: (jnp.maximum((b * nt + t) * halo_blocks - 1, 0), c)),
                  pl.BlockSpec((GDN_CONV_K, tc), lambda b, t, c: (0, c))],
        out_specs=pl.BlockSpec((tb, tc), lambda b, t, c: (b * nt + t, c)),
        out_shape=jax.ShapeDtypeStruct((m, conv_dim), out_dtype),
        scratch_shapes=[pltpu.VMEM((tb + 8, tc), F32)],
        compiler_params=_cparams(("parallel", "parallel", "parallel"), 32),
        name="gdn_conv",
    )(proj, proj, conv_w)


def _gdn_kernel(q_ref, k_ref, v_ref, z_ref, col_ref, grow_ref, glast_ref, ng_ref, o_ref, s_ref):
    t = pl.program_id(2)

    @pl.when(t == 0)
    def _():
        s_ref[...] = jnp.zeros_like(s_ref)

    tb = q_ref.shape[0]
    per_step = tb // CHUNK
    d = GDN_DIM
    c2 = 2 * CHUNK
    row = lax.broadcasted_iota(jnp.int32, (CHUNK, c2), 0)
    lane = lax.broadcasted_iota(jnp.int32, (CHUNK, c2), 1)
    first_head = lane < CHUNK
    key_pos = jnp.where(first_head, lane, lane - CHUNK)
    incl = key_pos <= row
    strict = key_pos < row
    nt = (((1,), (1,)), ((), ()))
    tn = (((0,), (0,)), ((), ()))
    zeros_vd = jnp.zeros((CHUNK, d), BF16)

    def block_diag(m16):
        zero = jnp.zeros_like(m16)
        return jnp.concatenate([jnp.where(first_head, m16, zero), jnp.where(first_head, zero, m16)], axis=0)

    for c in range(per_step):
        rows = slice(c * CHUNK, (c + 1) * CHUNK)
        q = q_ref[rows, :].astype(F32) * (d ** -0.5)
        k = k_ref[rows, :].astype(F32)
        v = v_ref[rows, :].astype(F32)
        cols = col_ref[rows, :]
        beta_a, beta_b = cols[:, 0:1], cols[:, 1:2]
        g_a, g_b = cols[:, 2:3], cols[:, 3:4]
        chunk_row = pl.ds(t * per_step + c, 1)
        g_row = grow_ref[chunk_row, :]
        g_end = glast_ref[chunk_row, :]

        kb16 = k.astype(BF16)
        lhs = jnp.concatenate([kb16, q.astype(BF16)], axis=0)
        rhs = jnp.concatenate([kb16, kb16], axis=0)
        prod = lax.dot_general(lhs, rhs, nt, preferred_element_type=F32)
        kk2, qk2 = prod[:CHUNK], prod[CHUNK:]

        g_col2 = jnp.where(first_head, g_a, g_b)
        beta2 = jnp.where(first_head, beta_a, beta_b)
        decay2 = jnp.where(incl, jnp.exp(jnp.where(incl, g_col2 - g_row, 0.0)), 0.0)
        x = jnp.where(strict, -(beta2 * kk2 * decay2), 0.0)
        qk_dec = (qk2 * decay2).astype(BF16)

        r = jnp.zeros((CHUNK, c2), F32)
        p = jnp.where((row >> 3) == (key_pos >> 3), x, 0.0)
        for _ in range(3):
            p16 = p.astype(BF16)
            y = jnp.dot(jnp.concatenate([r.astype(BF16), p16], axis=0), block_diag(p16),
                        preferred_element_type=F32)
            r = r + p + y[:CHUNK]
            p = y[CHUNK:]
        tinv = r + jnp.where(key_pos == row, 1.0, 0.0)
        for shift in (3, 4, 5):
            couples = jnp.logical_and((row >> (shift + 1)) == (key_pos >> (shift + 1)),
                                      (row >> shift) != (key_pos >> shift))
            xc16 = jnp.where(couples, x, 0.0).astype(BF16)
            t16 = tinv.astype(BF16)
            y = jnp.dot(xc16, block_diag(t16), preferred_element_type=F32)
            tinv = tinv + jnp.dot(t16, block_diag(y.astype(BF16)), preferred_element_type=F32)
        r = jnp.where(key_pos == row, 0.0, tinv)

        eg_a, eg_b = jnp.exp(g_a), jnp.exp(g_b)
        vb_a, vb_b = v[:, :d] * beta_a, v[:, d:] * beta_b
        kg_a, kg_b = k * (beta_a * eg_a), k * (beta_b * eg_b)
        rhs2 = jnp.concatenate([jnp.concatenate([vb_a, kg_a], axis=1),
                                jnp.concatenate([vb_b, kg_b], axis=1)], axis=0).astype(BF16)
        corr = jnp.dot(block_diag(r.astype(BF16)), rhs2, preferred_element_type=F32)
        u_a, w_a = vb_a + corr[:CHUNK, :d], kg_a + corr[:CHUNK, d:]
        u_b, w_b = vb_b + corr[CHUNK:, :d], kg_b + corr[CHUNK:, d:]

        s = s_ref[...]
        s16 = s.astype(BF16)
        ws = jnp.dot(jnp.concatenate([w_a, w_b], axis=0).astype(BF16), s16, preferred_element_type=F32)
        vn_a = u_a - ws[:CHUNK, :d]
        vn_b = u_b - ws[CHUNK:, d:]
        qs = jnp.dot(q.astype(BF16), s16, preferred_element_type=F32)
        vn_a16, vn_b16 = vn_a.astype(BF16), vn_b.astype(BF16)
        vn_bd = jnp.concatenate([jnp.concatenate([vn_a16, zeros_vd], axis=1),
                                 jnp.concatenate([zeros_vd, vn_b16], axis=1)], axis=0)
        intra = jnp.dot(qk_dec, vn_bd, preferred_element_type=F32)
        o_a = eg_a * qs[:, :d] + intra[:, :d]
        o_b = eg_b * qs[:, d:] + intra[:, d:]

        ed_a = jnp.exp(g_end[:, :d] - g_a)
        ed_b = jnp.exp(g_end[:, d:] - g_b)
        vn_dec = jnp.concatenate([vn_a * ed_a, vn_b * ed_b], axis=1).astype(BF16)
        s_ref[...] = s * jnp.exp(g_end) + lax.dot_general(kb16, vn_dec, tn, preferred_element_type=F32)

        z = z_ref[rows, :]
        o_ref[rows, :d] = (_rms_rows(o_a, ng_ref[...]) * _silu(z[:, :d])).astype(o_ref.dtype)
        o_ref[rows, d:] = (_rms_rows(o_b, ng_ref[...]) * _silu(z[:, d:])).astype(o_ref.dtype)


def gdn_attention(qkv, proj, colv, grow, glast, norm_g, *, batch, seq, tb):
    m = qkv.shape[0]
    nt = seq // tb
    nc = seq // CHUNK
    v0 = 2 * GDN_KH * GDN_DIM // (2 * GDN_DIM)
    z0 = (2 * GDN_KH + GDN_VH) * GDN_DIM // (2 * GDN_DIM)
    return pl.pallas_call(
        _gdn_kernel,
        grid=(batch, GDN_KH, nt),
        in_specs=[pl.BlockSpec((tb, GDN_DIM), lambda b, h, t: (b * nt + t, h)),
                  pl.BlockSpec((tb, GDN_DIM), lambda b, h, t: (b * nt + t, GDN_KH + h)),
                  pl.BlockSpec((tb, 2 * GDN_DIM), lambda b, h, t: (b * nt + t, v0 + h)),
                  pl.BlockSpec((tb, 2 * GDN_DIM), lambda b, h, t: (b * nt + t, z0 + h)),
                  pl.BlockSpec((None, None, tb, 4), lambda b, h, t: (b, h, t, 0)),
                  pl.BlockSpec((None, None, nc, 2 * CHUNK), lambda b, h, t: (b, h, 0, 0)),
                  pl.BlockSpec((None, None, nc, 2 * GDN_DIM), lambda b, h, t: (b, h, 0, 0)),
                  pl.BlockSpec((1, GDN_DIM), lambda b, h, t: (0, 0))],
        out_specs=pl.BlockSpec((tb, 2 * GDN_DIM), lambda b, h, t: (b * nt + t, h)),
        out_shape=jax.ShapeDtypeStruct((m, GDN_VH * GDN_DIM), BF16),
        scratch_shapes=[pltpu.VMEM((GDN_DIM, 2 * GDN_DIM), F32)],
        compiler_params=_cparams(("parallel", "parallel", "arbitrary"), 32),
        name="gdn_attention",
    )(qkv, qkv, qkv, proj, colv, grow, glast, norm_g)


def _pad_cols(w, n):
    return jnp.pad(w, ((0, 0), (0, n - w.shape[1])))


def _even_mixer(h, gpre, gpost, w_in, gate_w2, gate_b, gla_norm_g, w_out, *, batch, seq):
    n_main = w_in.shape[1] - GLA_RANK
    proj = norm_matmul(h, gpre, w_in[:, :n_main].astype(BF16), tm=1024, tn=1024, out_dtype=F32)
    w1 = _pad_cols(w_in[:, n_main:], LANES).astype(BF16)
    w2 = jnp.pad(gate_w2, ((0, LANES - GLA_RANK), (0, 0))).astype(BF16)
    gc = gla_gate(h, gpre, w1, w2, gate_b[None, :], tm=256)
    o_sb = sb_attention(proj, batch=batch, seq=seq, tq=128)
    o_gla = gla_attention(proj, gc, gla_norm_g[None, :], batch=batch, seq=seq, tb=256)
    o = jnp.concatenate([o_sb, o_gla], axis=-1)
    return matmul_post(o, w_out.astype(BF16), gpost, h, tm=512)


def _odd_mixer(h, gpre, gpost, w_in, conv_w, a_log, dt_bias, norm_g, w_out, *, batch, seq):
    n_main = w_in.shape[1] - 2 * GDN_VH
    conv_dim = conv_w.shape[1]
    proj = norm_matmul(h, gpre, w_in[:, :n_main].astype(BF16), tm=1024, tn=1024, out_dtype=F32)
    w_ba = _pad_cols(w_in[:, n_main:], LANES).astype(BF16)
    pad_gate = lambda p: jnp.pad(p, (GDN_VH, LANES - 2 * GDN_VH))[None, :]
    gates = gdn_gate(h, gpre, w_ba, pad_gate(a_log), pad_gate(dt_bias), tm=256)
    qkv = gdn_conv(proj, conv_w, batch=batch, seq=seq, tb=256, tc=512, out_dtype=F32)

    nc = seq // CHUNK
    beta = gates[:, :GDN_VH].reshape(batch, seq, GDN_KH, 2)
    gcum = gates[:, GDN_VH:2 * GDN_VH].reshape(batch, seq, GDN_KH, 2)
    colv = jnp.concatenate([beta, gcum], axis=-1).transpose(0, 2, 1, 3)
    gch = gcum.reshape(batch, nc, CHUNK, GDN_KH, 2)
    grow = gch.transpose(0, 3, 1, 4, 2).reshape(batch, GDN_KH, nc, 2 * CHUNK)
    glast = jnp.repeat(gch[:, :, CHUNK - 1].transpose(0, 2, 1, 3), GDN_DIM, axis=-1)

    o = gdn_attention(qkv, proj, colv, grow, glast, norm_g[None, :], batch=batch, seq=seq, tb=256)
    return matmul_post(o, w_out.astype(BF16), gpost, h, tm=512)


def kernel(x, mem, mem_norm_g, mem_w_kv, mix_pre_g, mix_post_g, ab_w_in, gla_gate_w2, gla_gate_b, gla_norm_g, ab_w_out, gdn_w_in, gdn_conv_w, gdn_a_log, gdn_dt_bias, gdn_norm_g, gdn_w_out, xattn_pre_g, xattn_post_g, xattn_w_q, xattn_w_o, ffn_pre_g, ffn_post_g, ffn_w_gate_up, ffn_w_down):
    batch, seq, d = x.shape
    depth = mix_pre_g.shape[0]
    mem_tokens = mem.shape[1]
    row = lambda g: g[None, :]

    kv = norm_matmul(mem.reshape(batch * mem_tokens, d), row(mem_norm_g), mem_w_kv.astype(BF16),
                     tm=batch * mem_tokens, tn=1024, out_dtype=BF16)

    h = x.reshape(batch * seq, d)
    for layer in range(depth):
        j = layer // 2
        gpre, gpost = row(mix_pre_g[layer]), row(mix_post_g[layer])
        if layer % 2 == 0:
            h = _even_mixer(h, gpre, gpost, ab_w_in[j], gla_gate_w2[j], gla_gate_b[j], gla_norm_g[j],
                            ab_w_out[j], batch=batch, seq=seq)
        else:
            h = _odd_mixer(h, gpre, gpost, gdn_w_in[j], gdn_conv_w[j], gdn_a_log[j], gdn_dt_bias[j],
                           gdn_norm_g[j], gdn_w_out[j], batch=batch, seq=seq)
        h = xattn_layer(h, row(xattn_pre_g[layer]), xattn_w_q[layer].astype(BF16), kv,
                        xattn_w_o[layer].astype(BF16), row(xattn_post_g[layer]), tm=512, seq=seq)
        h = ffn_layer(h, row(ffn_pre_g[layer]), ffn_w_gate_up[layer].astype(BF16),
                      ffn_w_down[layer].astype(BF16), row(ffn_post_g[layer]), tm=512, tf=512)
    return h.reshape(batch, seq, d)
```

```python
import functools

import jax
import jax.numpy as jnp
from jax import lax
from jax.experimental import pallas as pl
from jax.experimental.pallas import tpu as pltpu

F32 = jnp.float32
BF16 = jnp.bfloat16
RMS_EPS = 1e-6

SB_HEADS = 8
SB_DIM = 128
GLA_HEADS = 4
GLA_DK = 128
GLA_DV = 256
GLA_RANK = 16
GLA_NORMALIZER = 16.0
GDN_KH = 16
GDN_VH = 32
GDN_DIM = 128
GDN_CONV_K = 4
MEM_HEADS = 4
CHUNK = 64
LANES = 128
ROW_CHUNK = 64
SB_LOG_CUTOFF = -104.0


def _cparams(semantics, vmem_mib):
    return pltpu.CompilerParams(dimension_semantics=semantics, vmem_limit_bytes=vmem_mib << 20)


def _rms_rows(x, g):
    ms = jnp.mean(x * x, axis=-1, keepdims=True)
    return x * lax.rsqrt(ms + RMS_EPS) * g


def _softplus(z):
    return jnp.maximum(z, 0.0) + jnp.log1p(jnp.exp(-jnp.abs(z)))


def _silu(x):
    return x * jax.nn.sigmoid(x)


def _split3(x):
    hi = x.astype(BF16)
    r1 = x - hi.astype(F32)
    mid = r1.astype(BF16)
    lo = (r1 - mid.astype(F32)).astype(BF16)
    return hi, mid, lo


def _for_rows(n_rows, body):
    def step(r, c):
        body(pl.ds(pl.multiple_of(r * ROW_CHUNK, ROW_CHUNK), ROW_CHUNK))
        return c
    lax.fori_loop(0, n_rows // ROW_CHUNK, step, 0)


def _prenorm_to(xn_ref, h_ref, g_ref):
    def body(rows):
        xn_ref[rows, :] = _rms_rows(h_ref[rows, :], g_ref[...]).astype(xn_ref.dtype)
    _for_rows(h_ref.shape[0], body)


def _postnorm_residual_to(o_ref, h_ref, acc_ref, g_ref):
    def body(rows):
        o_ref[rows, :] = h_ref[rows, :] + _rms_rows(acc_ref[rows, :], g_ref[...])
    _for_rows(h_ref.shape[0], body)


def _norm_matmul_kernel(h_ref, g_ref, w_ref, o_ref, xn_ref):
    @pl.when(pl.program_id(1) == 0)
    def _():
        _prenorm_to(xn_ref, h_ref, g_ref)
    o_ref[...] = jnp.dot(xn_ref[...], w_ref[...], preferred_element_type=F32).astype(o_ref.dtype)


def norm_matmul(h, g, w, *, tm, tn, out_dtype):
    m, d = h.shape
    n = w.shape[1]
    return pl.pallas_call(
        _norm_matmul_kernel,
        grid=(m // tm, n // tn),
        in_specs=[pl.BlockSpec((tm, d), lambda i, j: (i, 0)),
                  pl.BlockSpec((1, d), lambda i, j: (0, 0)),
                  pl.BlockSpec((d, tn), lambda i, j: (0, j))],
        out_specs=pl.BlockSpec((tm, tn), lambda i, j: (i, j)),
        out_shape=jax.ShapeDtypeStruct((m, n), out_dtype),
        scratch_shapes=[pltpu.VMEM((tm, d), BF16)],
        compiler_params=_cparams(("parallel", "arbitrary"), 48),
        name="norm_matmul",
    )(h, g, w)


def _matmul_post_kernel(*refs):
    a_refs, (w_ref, g_ref, h_ref, o_ref, acc_ref) = refs[:-5], refs[-5:]
    k0 = 0
    for idx, a_ref in enumerate(a_refs):
        kw = a_ref.shape[1]
        part = jnp.dot(a_ref[...], w_ref[k0:k0 + kw, :], preferred_element_type=F32)
        if idx == 0:
            acc_ref[...] = part
        else:
            acc_ref[...] += part
        k0 += kw
    _postnorm_residual_to(o_ref, h_ref, acc_ref, g_ref)


def matmul_post(a_slabs, w, g, h, *, tm):
    m = h.shape[0]
    k, d = w.shape
    return pl.pallas_call(
        _matmul_post_kernel,
        grid=(m // tm,),
        in_specs=[pl.BlockSpec((tm, a.shape[1]), lambda i: (i, 0)) for a in a_slabs] + [
                  pl.BlockSpec((k, d), lambda i: (0, 0), pipeline_mode=pl.Buffered(1)),
                  pl.BlockSpec((1, d), lambda i: (0, 0)),
                  pl.BlockSpec((tm, d), lambda i: (i, 0))],
        out_specs=pl.BlockSpec((tm, d), lambda i: (i, 0)),
        out_shape=jax.ShapeDtypeStruct((m, d), F32),
        scratch_shapes=[pltpu.VMEM((tm, d), F32)],
        compiler_params=_cparams(("parallel",), ((2 * tm * k * 2 + k * d * 2 + 5 * tm * d * 4) >> 20) + 8),
        name="matmul_post",
    )(*a_slabs, w, g, h)


def _ffn_kernel(h_ref, gpre_ref, wg_ref, wu_ref, wd_ref, gpost_ref, o_ref, xn_ref, acc_ref):
    j = pl.program_id(1)

    @pl.when(j == 0)
    def _():
        _prenorm_to(xn_ref, h_ref, gpre_ref)
        acc_ref[...] = jnp.zeros_like(acc_ref)

    xn = xn_ref[...]
    gate = jnp.dot(xn, wg_ref[...], preferred_element_type=F32)
    up = jnp.dot(xn, wu_ref[...], preferred_element_type=F32)
    act = (_silu(gate) * up).astype(BF16)
    acc_ref[...] += jnp.dot(act, wd_ref[...], preferred_element_type=F32)

    @pl.when(j == pl.num_programs(1) - 1)
    def _():
        _postnorm_residual_to(o_ref, h_ref, acc_ref, gpost_ref)


def ffn_layer(h, gpre, w_gate_up, w_down, gpost, *, tm, tf):
    m, d = h.shape
    f = w_down.shape[0]
    nf = f // tf
    return pl.pallas_call(
        _ffn_kernel,
        grid=(m // tm, nf),
        in_specs=[pl.BlockSpec((tm, d), lambda i, j: (i, 0)),
                  pl.BlockSpec((1, d), lambda i, j: (0, 0)),
                  pl.BlockSpec((d, tf), lambda i, j: (0, j)),
                  pl.BlockSpec((d, tf), lambda i, j: (0, j + nf)),
                  pl.BlockSpec((tf, d), lambda i, j: (j, 0)),
                  pl.BlockSpec((1, d), lambda i, j: (0, 0))],
        out_specs=pl.BlockSpec((tm, d), lambda i, j: (i, 0)),
        out_shape=jax.ShapeDtypeStruct((m, d), F32),
        scratch_shapes=[pltpu.VMEM((tm, d), BF16), pltpu.VMEM((tm, d), F32)],
        compiler_params=_cparams(("parallel", "arbitrary"), 48),
        name="ffn_layer",
    )(h, gpre, w_gate_up, w_gate_up, w_down, gpost)


def _xattn_kernel(h_ref, gpre_ref, wq_ref, k_ref, v_ref, wo_ref, gpost_ref, o_ref, xn_ref, att_ref, acc_ref):
    _prenorm_to(xn_ref, h_ref, gpre_ref)
    d = h_ref.shape[1]
    hd = d // MEM_HEADS
    xn = xn_ref[...]
    for i in range(MEM_HEADS):
        cols = slice(i * hd, (i + 1) * hd)
        q = jnp.dot(xn, wq_ref[:, cols], preferred_element_type=F32)
        s = lax.dot_general(q.astype(BF16), k_ref[:, cols], (((1,), (1,)), ((), ())),
                            preferred_element_type=F32) * (hd ** -0.5)
        e = jnp.exp(s - jnp.max(s, axis=-1, keepdims=True))
        p = e / jnp.sum(e, axis=-1, keepdims=True)
        att_ref[:, cols] = jnp.dot(p.astype(BF16), v_ref[:, cols], preferred_element_type=F32).astype(BF16)
    acc_ref[...] = jnp.dot(att_ref[...], wo_ref[...], preferred_element_type=F32)
    _postnorm_residual_to(o_ref, h_ref, acc_ref, gpost_ref)


def xattn_layer(h, gpre, wq, kv, wo, gpost, *, tm, seq):
    m, d = h.shape
    mt = kv.shape[0] // (m // seq)
    per_b = seq // tm
    once = pl.Buffered(1)
    vmem = 4 * tm * d * 4 + 2 * d * d * 2 + 4 * mt * d * 2 + 2 * tm * d * 2 + tm * d * 4
    return pl.pallas_call(
        _xattn_kernel,
        grid=(m // tm,),
        in_specs=[pl.BlockSpec((tm, d), lambda i: (i, 0)),
                  pl.BlockSpec((1, d), lambda i: (0, 0)),
                  pl.BlockSpec((d, d), lambda i: (0, 0), pipeline_mode=once),
                  pl.BlockSpec((mt, d), lambda i: (i // per_b, 0)),
                  pl.BlockSpec((mt, d), lambda i: (i // per_b, 1)),
                  pl.BlockSpec((d, d), lambda i: (0, 0), pipeline_mode=once),
                  pl.BlockSpec((1, d), lambda i: (0, 0))],
        out_specs=pl.BlockSpec((tm, d), lambda i: (i, 0)),
        out_shape=jax.ShapeDtypeStruct((m, d), F32),
        scratch_shapes=[pltpu.VMEM((tm, d), BF16), pltpu.VMEM((tm, d), BF16), pltpu.VMEM((tm, d), F32)],
        compiler_params=_cparams(("parallel",), (vmem >> 20) + 10),
        name="xattn_layer",
    )(h, gpre, wq, kv, kv, wo, gpost)


def _split_heads(x, n, width):
    return jnp.stack([x[:, i * width:(i + 1) * width] for i in range(n)], axis=0)


def _bmm(a, b):
    return lax.dot_general(a, b, (((2,), (1,)), ((0,), (0,))), preferred_element_type=F32)


def _bmm_nt(a, b):
    return lax.dot_general(a, b, (((2,), (2,)), ((0,), (0,))), preferred_element_type=F32)


def _sb_kernel(q_ref, k_ref, v_ref, o_ref, carry_ref, acc_ref):
    tq = q_ref.shape[0]
    nh = SB_HEADS
    qi = pl.program_id(1)
    scale = SB_DIM ** -0.5
    q = _split_heads(q_ref[...], nh, SB_DIM)
    row = lax.broadcasted_iota(jnp.int32, (tq, tq), 0)
    col = lax.broadcasted_iota(jnp.int32, (tq, tq), 1)
    after = jnp.where(row > col, 1.0, 0.0).astype(BF16)

    def block(j, diagonal):
        ks = pl.ds(pl.multiple_of(j * tq, tq), tq)
        kb = _split_heads(k_ref[ks, :], nh, SB_DIM)
        vb = _split_heads(v_ref[ks, :], nh, SB_DIM)
        z = _bmm_nt(q, kb) * scale
        sp = _softplus(z)
        log_keep = -sp
        if diagonal:
            valid = col < row
            log_keep = jnp.where(valid, log_keep, 0.0)
        hi, mid, lo = _split3(log_keep.reshape(nh * tq, tq))
        later = (jnp.dot(hi, after, preferred_element_type=F32)
                 + jnp.dot(mid, after, preferred_element_type=F32)
                 + jnp.dot(lo, after, preferred_element_type=F32)).reshape(nh, tq, tq)
        w = jnp.exp(z - sp + later + carry_ref[...])
        if diagonal:
            w = jnp.where(valid, w, 0.0)
        acc_ref[...] += _bmm(w.astype(BF16), vb)
        carry_ref[...] += jnp.sum(log_keep, axis=-1, keepdims=True)
        return jnp.max(carry_ref[...])

    carry_ref[...] = jnp.zeros_like(carry_ref)
    acc_ref[...] = jnp.zeros_like(acc_ref)
    top = block(qi, True)

    def cond(state):
        j, top = state
        return jnp.logical_and(j >= 0, top > SB_LOG_CUTOFF)

    def body(state):
        j, _ = state
        return j - 1, block(j, False)

    lax.while_loop(cond, body, (qi - 1, top))
    for i in range(nh):
        o_ref[:, i * SB_DIM:(i + 1) * SB_DIM] = acc_ref[i].astype(o_ref.dtype)


def sb_attention(qkv, *, batch, seq, tq):
    m = qkv.shape[0]
    nq = seq // tq
    width = SB_HEADS * SB_DIM
    return pl.pallas_call(
        _sb_kernel,
        grid=(batch, nq),
        in_specs=[pl.BlockSpec((tq, width), lambda b, i: (b * nq + i, 0)),
                  pl.BlockSpec((seq, width), lambda b, i: (b, 1)),
                  pl.BlockSpec((seq, width), lambda b, i: (b, 2))],
        out_specs=pl.BlockSpec((tq, width), lambda b, i: (b * nq + i, 0)),
        out_shape=jax.ShapeDtypeStruct((m, width), BF16),
        scratch_shapes=[pltpu.VMEM((SB_HEADS, tq, 1), F32), pltpu.VMEM((SB_HEADS, tq, SB_DIM), F32)],
        compiler_params=_cparams(("parallel", "arbitrary"), 32),
        name="sb_attention",
    )(qkv, qkv, qkv)


def _chunk_cumsum_matrix(n):
    row = lax.broadcasted_iota(jnp.int32, (n, n), 0)
    col = lax.broadcasted_iota(jnp.int32, (n, n), 1)
    same = (row // CHUNK) == (col // CHUNK)
    return jnp.where(jnp.logical_and(same, col <= row), 1.0, 0.0).astype(BF16)


def _chunk_cumsum(x):
    tri = _chunk_cumsum_matrix(x.shape[0])
    hi, mid, lo = _split3(x)
    return (jnp.dot(tri, hi, preferred_element_type=F32)
            + jnp.dot(tri, mid, preferred_element_type=F32)
            + jnp.dot(tri, lo, preferred_element_type=F32))


def _gla_gate_kernel(h_ref, gpre_ref, w1_ref, w2_ref, b_ref, o_ref):
    xn = _rms_rows(h_ref[...], gpre_ref[...]).astype(BF16)
    low = jnp.dot(xn, w1_ref[...], preferred_element_type=F32)
    pre = jnp.dot(low.astype(BF16), w2_ref[...], preferred_element_type=F32) + b_ref[...]
    log_a = -_softplus(-pre) * (1.0 / GLA_NORMALIZER)
    o_ref[...] = _chunk_cumsum(log_a)


def gla_gate(h, gpre, w1, w2, b, *, tm):
    m, d = h.shape
    n = w2.shape[1]
    return pl.pallas_call(
        _gla_gate_kernel,
        grid=(m // tm,),
        in_specs=[pl.BlockSpec((tm, d), lambda i: (i, 0)),
                  pl.BlockSpec((1, d), lambda i: (0, 0)),
                  pl.BlockSpec((d, LANES), lambda i: (0, 0)),
                  pl.BlockSpec((LANES, n), lambda i: (0, 0)),
                  pl.BlockSpec((1, n), lambda i: (0, 0))],
        out_specs=pl.BlockSpec((tm, n), lambda i: (i, 0)),
        out_shape=jax.ShapeDtypeStruct((m, n), F32),
        compiler_params=_cparams(("parallel",), 32),
        name="gla_gate",
    )(h, gpre, w1, w2, b)


def _gla_kernel(q_ref, k_ref, v_ref, r_ref, gc_ref, ng_ref, o_ref, st_ref):
    @pl.when(pl.program_id(2) == 0)
    def _():
        st_ref[...] = jnp.zeros_like(st_ref)

    tb = q_ref.shape[0]
    row = lax.broadcasted_iota(jnp.int32, (CHUNK, CHUNK), 0)
    col = lax.broadcasted_iota(jnp.int32, (CHUNK, CHUNK), 1)
    causal = col <= row
    nt = (((1,), (1,)), ((), ()))
    tn = (((0,), (0,)), ((), ()))
    for c in range(tb // CHUNK):
        rows = slice(c * CHUNK, (c + 1) * CHUNK)
        g = gc_ref[rows, :]
        g_last = g[CHUNK - 1:CHUNK, :]
        k = k_ref[rows, :]
        v = v_ref[rows, :].astype(BF16)
        q_dec = (q_ref[rows, :] * (GLA_DK ** -0.5) * jnp.exp(g)).astype(BF16)
        k_inv = (k * jnp.exp(-g)).astype(BF16)
        k_dec = (k * jnp.exp(g_last - g)).astype(BF16)
        scores = lax.dot_general(q_dec, k_inv, nt, preferred_element_type=F32)
        scores = jnp.where(causal, scores, 0.0).astype(BF16)
        st = st_ref[...]
        o = (jnp.dot(scores, v, preferred_element_type=F32)
             + lax.dot_general(q_dec, st.astype(BF16), nt, preferred_element_type=F32))
        st_ref[...] = st * jnp.exp(g_last) + lax.dot_general(v, k_dec, tn, preferred_element_type=F32)
        o_ref[rows, :] = (_rms_rows(o, ng_ref[...]) * _silu(r_ref[rows, :])).astype(o_ref.dtype)


def gla_attention(proj, gc, norm_g, *, batch, seq, tb):
    m = proj.shape[0]
    nt = seq // tb
    qk0 = 0
    v0 = 2 * GLA_HEADS * GLA_DK // GLA_DV
    return pl.pallas_call(
        _gla_kernel,
        grid=(batch, GLA_HEADS, nt),
        in_specs=[pl.BlockSpec((tb, GLA_DK), lambda b, h, t: (b * nt + t, qk0 + h)),
                  pl.BlockSpec((tb, GLA_DK), lambda b, h, t: (b * nt + t, qk0 + GLA_HEADS + h)),
                  pl.BlockSpec((tb, GLA_DV), lambda b, h, t: (b * nt + t, v0 + h)),
                  pl.BlockSpec((tb, GLA_DV), lambda b, h, t: (b * nt + t, v0 + GLA_HEADS + h)),
                  pl.BlockSpec((tb, GLA_DK), lambda b, h, t: (b * nt + t, h)),
                  pl.BlockSpec((1, GLA_DV), lambda b, h, t: (0, 0))],
        out_specs=pl.BlockSpec((tb, GLA_DV), lambda b, h, t: (b * nt + t, h)),
        out_shape=jax.ShapeDtypeStruct((m, GLA_HEADS * GLA_DV), BF16),
        scratch_shapes=[pltpu.VMEM((GLA_DV, GLA_DK), F32)],
        compiler_params=_cparams(("parallel", "parallel", "arbitrary"), 32),
        name="gla_attention",
    )(proj, proj, proj, proj, gc, norm_g)


def _gdn_gate_kernel(h_ref, gpre_ref, w_ref, alog_ref, dtb_ref, o_ref):
    xn = _rms_rows(h_ref[...], gpre_ref[...]).astype(BF16)
    ba = jnp.dot(xn, w_ref[...], preferred_element_type=F32)
    beta = jax.nn.sigmoid(ba)
    g = -jnp.exp(alog_ref[...]) * _softplus(ba + dtb_ref[...])
    lane = lax.broadcasted_iota(jnp.int32, ba.shape, 1)
    o_ref[...] = jnp.where(lane < GDN_VH, beta, _chunk_cumsum(g))


def gdn_gate(h, gpre, w, alog, dtb, *, tm):
    m, d = h.shape
    return pl.pallas_call(
        _gdn_gate_kernel,
        grid=(m // tm,),
        in_specs=[pl.BlockSpec((tm, d), lambda i: (i, 0)),
                  pl.BlockSpec((1, d), lambda i: (0, 0)),
                  pl.BlockSpec((d, LANES), lambda i: (0, 0)),
                  pl.BlockSpec((1, LANES), lambda i: (0, 0)),
                  pl.BlockSpec((1, LANES), lambda i: (0, 0))],
        out_specs=pl.BlockSpec((tm, LANES), lambda i: (i, 0)),
        out_shape=jax.ShapeDtypeStruct((m, LANES), F32),
        compiler_params=_cparams(("parallel",), 32),
        name="gdn_gate",
    )(h, gpre, w, alog, dtb)


def _gdn_conv_kernel(x_ref, prev_ref, w_ref, o_ref, xs_ref, *, l2_normalize):
    tb, tc = x_ref.shape
    halo = 8
    first = pl.program_id(1) == 0
    xs_ref[0:halo, :] = jnp.where(first, 0.0, prev_ref[...])
    xs_ref[halo:halo + tb, :] = x_ref[...]
    for r in range(tb // ROW_CHUNK):
        for hh in range(tc // GDN_DIM):
            cols = slice(hh * GDN_DIM, (hh + 1) * GDN_DIM)
            y = jnp.zeros((ROW_CHUNK, GDN_DIM), F32)
            for i in range(GDN_CONV_K):
                off = r * ROW_CHUNK + halo - (GDN_CONV_K - 1) + i
                y = y + w_ref[i:i + 1, cols] * xs_ref[off:off + ROW_CHUNK, cols]
            y = _silu(y)
            if l2_normalize:
                y = y * lax.rsqrt(jnp.sum(y * y, axis=-1, keepdims=True) + RMS_EPS)
            o_ref[r * ROW_CHUNK:(r + 1) * ROW_CHUNK, cols] = y.astype(o_ref.dtype)


def gdn_conv(proj, conv_w, *, col0, width, l2_normalize, batch, seq, tb, tc):
    m = proj.shape[0]
    nt = seq // tb
    halo_blocks = tb // 8
    c0 = col0 // tc
    return pl.pallas_call(
        functools.partial(_gdn_conv_kernel, l2_normalize=l2_normalize),
        grid=(batch, nt, width // tc),
        in_specs=[pl.BlockSpec((tb, tc), lambda b, t, c: (b * nt + t, c0 + c)),
                  pl.BlockSpec((8, tc), lambda b, t, c: (jnp.maximum((b * nt + t) * halo_blocks - 1, 0), c0 + c)),
                  pl.BlockSpec((GDN_CONV_K, tc), lambda b, t, c: (0, c0 + c))],
        out_specs=pl.BlockSpec((tb, tc), lambda b, t, c: (b * nt + t, c)),
        out_shape=jax.ShapeDtypeStruct((m, width), F32),
        scratch_shapes=[pltpu.VMEM((tb + 8, tc), F32)],
        compiler_params=_cparams(("parallel", "parallel", "parallel"), 32),
        name="gdn_conv",
    )(proj, proj, conv_w)


def _gdn_kernel(q_ref, k_ref, v_ref, z_ref, col_ref, grow_ref, glast_ref, ng_ref, o_ref,
                s_ref, u_ref, w_ref, qkd_ref, q16_ref, k16_ref, eg_ref, ed_ref, egl_ref):
    @pl.when(pl.program_id(2) == 0)
    def _():
        s_ref[...] = jnp.zeros_like(s_ref)

    groups, nc = u_ref.shape[0], u_ref.shape[1]
    n = groups * nc
    d = GDN_DIM
    c2 = 2 * CHUNK
    row = lax.broadcasted_iota(jnp.int32, (CHUNK, c2), 0)
    lane = lax.broadcasted_iota(jnp.int32, (CHUNK, c2), 1)
    first_head = lane < CHUNK
    key_pos = jnp.where(first_head, lane, lane - CHUNK)
    incl = key_pos <= row
    strict = key_pos < row
    diag = key_pos == row

    def chunked(ref, width):
        return jnp.concatenate([ref[:, g * width:(g + 1) * width].reshape(nc, CHUNK, width)
                                for g in range(groups)], axis=0)

    def block_diag(m16):
        zero = jnp.zeros_like(m16)
        return jnp.concatenate([jnp.where(first_head, m16, zero), jnp.where(first_head, zero, m16)], axis=1)

    q = chunked(q_ref, d).astype(F32) * (d ** -0.5)
    k = chunked(k_ref, d).astype(F32)
    v = chunked(v_ref, 2 * d).astype(F32)
    cols = jnp.concatenate([col_ref[g].reshape(nc, CHUNK, 4) for g in range(groups)], axis=0)
    beta_a, beta_b = cols[:, :, 0:1], cols[:, :, 1:2]
    g_a, g_b = cols[:, :, 2:3], cols[:, :, 3:4]
    g_row = grow_ref[...].reshape(n, 1, c2)
    g_end = glast_ref[...].reshape(n, 1, 2 * d)

    k16 = k.astype(BF16)
    q16 = q.astype(BF16)
    prod = _bmm_nt(jnp.concatenate([k16, q16], axis=1), jnp.concatenate([k16, k16], axis=1))
    kk2, qk2 = prod[:, :CHUNK], prod[:, CHUNK:]

    g_col2 = jnp.where(first_head, g_a, g_b)
    beta2 = jnp.where(first_head, beta_a, beta_b)
    decay2 = jnp.where(incl, jnp.exp(jnp.where(incl, g_col2 - g_row, 0.0)), 0.0)
    x = jnp.where(strict, -(beta2 * kk2 * decay2), 0.0)
    qkd_ref[...] = (qk2 * decay2).astype(BF16).reshape(qkd_ref.shape)

    r = jnp.zeros_like(x)
    p = jnp.where((row >> 3) == (key_pos >> 3), x, 0.0)
    for _ in range(3):
        p16 = p.astype(BF16)
        y = _bmm(jnp.concatenate([r.astype(BF16), p16], axis=1), block_diag(p16))
        r = r + p + y[:, :CHUNK]
        p = y[:, CHUNK:]
    tinv = r + jnp.where(diag, 1.0, 0.0)
    for shift in (3, 4, 5):
        couples = jnp.logical_and((row >> (shift + 1)) == (key_pos >> (shift + 1)),
                                  (row >> shift) != (key_pos >> shift))
        xc16 = jnp.where(couples, x, 0.0).astype(BF16)
        t16 = tinv.astype(BF16)
        y = _bmm(xc16, block_diag(t16))
        tinv = tinv + _bmm(t16, block_diag(y.astype(BF16)))
    r = jnp.where(diag, 0.0, tinv)

    eg_a, eg_b = jnp.exp(g_a), jnp.exp(g_b)
    vb_a, vb_b = v[:, :, :d] * beta_a, v[:, :, d:] * beta_b
    kg_a, kg_b = k * (beta_a * eg_a), k * (beta_b * eg_b)
    rhs2 = jnp.concatenate([jnp.concatenate([vb_a, kg_a], axis=2),
                            jnp.concatenate([vb_b, kg_b], axis=2)], axis=1).astype(BF16)
    corr = _bmm(block_diag(r.astype(BF16)), rhs2)
    u_ref[...] = jnp.concatenate([vb_a + corr[:, :CHUNK, :d], vb_b + corr[:, CHUNK:, :d]],
                                 axis=2).reshape(u_ref.shape)
    w_ref[...] = jnp.concatenate([kg_a + corr[:, :CHUNK, d:], kg_b + corr[:, CHUNK:, d:]],
                                 axis=1).astype(BF16).reshape(w_ref.shape)
    q16_ref[...] = q16.reshape(q16_ref.shape)
    k16_ref[...] = k16.reshape(k16_ref.shape)
    lanes_d = (n, CHUNK, d)
    eg_ref[...] = jnp.concatenate([jnp.broadcast_to(eg_a, lanes_d), jnp.broadcast_to(eg_b, lanes_d)],
                                  axis=2).reshape(eg_ref.shape)
    ed_ref[...] = jnp.concatenate([jnp.exp(g_end[:, :, :d] - g_a), jnp.exp(g_end[:, :, d:] - g_b)],
                                  axis=2).reshape(ed_ref.shape)
    egl_ref[...] = jnp.exp(g_end).reshape(egl_ref.shape)

    zeros_vd = jnp.zeros((groups, CHUNK, d), BF16)
    tn = (((1,), (1,)), ((0,), (0,)))
    for c in range(nc):
        rows = slice(c * CHUNK, (c + 1) * CHUNK)
        s = s_ref[...]
        s16 = s.astype(BF16)
        ws = _bmm(jnp.concatenate([w_ref[:, c], q16_ref[:, c]], axis=1), s16)
        vn = u_ref[:, c] - jnp.concatenate([ws[:, :CHUNK, :d], ws[:, CHUNK:c2, d:]], axis=2)
        vn16 = vn.astype(BF16)
        vn_bd = jnp.concatenate([jnp.concatenate([vn16[:, :, :d], zeros_vd], axis=2),
                                 jnp.concatenate([zeros_vd, vn16[:, :, d:]], axis=2)], axis=1)
        o = eg_ref[:, c] * ws[:, c2:] + _bmm(qkd_ref[:, c], vn_bd)
        vn_dec = (vn * ed_ref[:, c]).astype(BF16)
        s_ref[...] = s * egl_ref[:, c] + lax.dot_general(k16_ref[:, c], vn_dec, tn,
                                                         preferred_element_type=F32)
        for g in range(groups):
            for e in range(2):
                cs = slice((2 * g + e) * d, (2 * g + e + 1) * d)
                o_ref[rows, cs] = (_rms_rows(o[g, :, e * d:(e + 1) * d], ng_ref[...])
                                   * _silu(z_ref[rows, cs])).astype(o_ref.dtype)


def gdn_attention(qk, vv, proj, colv, grow, glast, norm_g, *, batch, seq, tb, groups):
    m = qk.shape[0]
    nt = seq // tb
    nc = tb // CHUNK
    d = GDN_DIM
    kw, vw = groups * d, groups * 2 * d
    k0 = GDN_KH * d // kw
    z0 = (2 * GDN_KH + GDN_VH) * d // vw
    per = lambda *shape: (groups, nc) + shape
    return pl.pallas_call(
        _gdn_kernel,
        grid=(batch, GDN_KH // groups, nt),
        in_specs=[pl.BlockSpec((tb, kw), lambda b, h, t: (b * nt + t, h)),
                  pl.BlockSpec((tb, kw), lambda b, h, t: (b * nt + t, k0 + h)),
                  pl.BlockSpec((tb, vw), lambda b, h, t: (b * nt + t, h)),
                  pl.BlockSpec((tb, vw), lambda b, h, t: (b * nt + t, z0 + h)),
                  pl.BlockSpec((None, groups, tb, 4), lambda b, h, t: (b, h, t, 0)),
                  pl.BlockSpec((None, groups, nc, 1, 2 * CHUNK), lambda b, h, t: (b, h, t, 0, 0)),
                  pl.BlockSpec((None, groups, nc, 1, 2 * d), lambda b, h, t: (b, h, t, 0, 0)),
                  pl.BlockSpec((1, d), lambda b, h, t: (0, 0))],
        out_specs=pl.BlockSpec((tb, vw), lambda b, h, t: (b * nt + t, h)),
        out_shape=jax.ShapeDtypeStruct((m, GDN_VH * d), BF16),
        scratch_shapes=[pltpu.VMEM((groups, d, 2 * d), F32),
                        pltpu.VMEM(per(CHUNK, 2 * d), F32),
                        pltpu.VMEM(per(2 * CHUNK, d), BF16),
                        pltpu.VMEM(per(CHUNK, 2 * CHUNK), BF16),
                        pltpu.VMEM(per(CHUNK, d), BF16),
                        pltpu.VMEM(per(CHUNK, d), BF16),
                        pltpu.VMEM(per(CHUNK, 2 * d), F32),
                        pltpu.VMEM(per(CHUNK, 2 * d), F32),
                        pltpu.VMEM(per(1, 2 * d), F32)],
        compiler_params=_cparams(("parallel", "parallel", "arbitrary"), 32),
        name="gdn_attention",
    )(qk, qk, vv, proj, colv, grow, glast, norm_g)


def _pad_cols(w, n):
    return jnp.pad(w, ((0, 0), (0, n - w.shape[1])))


def _even_mixer(h, gpre, gpost, w_in, gate_w2, gate_b, gla_norm_g, w_out, *, batch, seq):
    n_sb = 3 * SB_HEADS * SB_DIM
    n_main = w_in.shape[1] - GLA_RANK
    sb_qkv = norm_matmul(h, gpre, w_in[:, :n_sb].astype(BF16), tm=1024, tn=1024, out_dtype=BF16)
    gla_proj = norm_matmul(h, gpre, w_in[:, n_sb:n_main].astype(BF16), tm=1024, tn=1024, out_dtype=F32)
    w1 = _pad_cols(w_in[:, n_main:], LANES).astype(BF16)
    w2 = jnp.pad(gate_w2, ((0, LANES - GLA_RANK), (0, 0))).astype(BF16)
    gc = gla_gate(h, gpre, w1, w2, gate_b[None, :], tm=256)
    o_sb = sb_attention(sb_qkv, batch=batch, seq=seq, tq=128)
    o_gla = gla_attention(gla_proj, gc, gla_norm_g[None, :], batch=batch, seq=seq, tb=256)
    return matmul_post((o_sb, o_gla), w_out.astype(BF16), gpost, h, tm=512)


def _odd_mixer(h, gpre, gpost, w_in, conv_w, a_log, dt_bias, norm_g, w_out, *, batch, seq):
    n_main = w_in.shape[1] - 2 * GDN_VH
    conv_dim = conv_w.shape[1]
    proj = norm_matmul(h, gpre, w_in[:, :n_main].astype(BF16), tm=1024, tn=1024, out_dtype=F32)
    w_ba = _pad_cols(w_in[:, n_main:], LANES).astype(BF16)
    pad_gate = lambda p: jnp.pad(p, (GDN_VH, LANES - 2 * GDN_VH))[None, :]
    gates = gdn_gate(h, gpre, w_ba, pad_gate(a_log), pad_gate(dt_bias), tm=256)
    qk_width = 2 * GDN_KH * GDN_DIM
    qk = gdn_conv(proj, conv_w, col0=0, width=qk_width, l2_normalize=True, batch=batch, seq=seq, tb=512, tc=1024)
    vv = gdn_conv(proj, conv_w, col0=qk_width, width=conv_dim - qk_width, l2_normalize=False,
                  batch=batch, seq=seq, tb=512, tc=1024)

    nc = seq // CHUNK
    beta = gates[:, :GDN_VH].reshape(batch, seq, GDN_KH, 2)
    gcum = gates[:, GDN_VH:2 * GDN_VH].reshape(batch, seq, GDN_KH, 2)
    colv = jnp.concatenate([beta, gcum], axis=-1).transpose(0, 2, 1, 3)
    gch = gcum.reshape(batch, nc, CHUNK, GDN_KH, 2)
    grow = gch.transpose(0, 3, 1, 4, 2).reshape(batch, GDN_KH, nc, 1, 2 * CHUNK)
    glast = jnp.repeat(gch[:, :, CHUNK - 1].transpose(0, 2, 1, 3), GDN_DIM, axis=-1)[:, :, :, None, :]

    o = gdn_attention(qk, vv, proj, colv, grow, glast, norm_g[None, :], batch=batch, seq=seq, tb=256, groups=4)
    return matmul_post((o,), w_out.astype(BF16), gpost, h, tm=512)


def kernel(x, mem, mem_norm_g, mem_w_kv, mix_pre_g, mix_post_g, ab_w_in, gla_gate_w2, gla_gate_b, gla_norm_g, ab_w_out, gdn_w_in, gdn_conv_w, gdn_a_log, gdn_dt_bias, gdn_norm_g, gdn_w_out, xattn_pre_g, xattn_post_g, xattn_w_q, xattn_w_o, ffn_pre_g, ffn_post_g, ffn_w_gate_up, ffn_w_down):
    batch, seq, d = x.shape
    depth = mix_pre_g.shape[0]
    mem_tokens = mem.shape[1]
    row = lambda g: g[None, :]

    kv = norm_matmul(mem.reshape(batch * mem_tokens, d), row(mem_norm_g), mem_w_kv.astype(BF16),
                     tm=batch * mem_tokens, tn=1024, out_dtype=BF16)

    h = x.reshape(batch * seq, d)
    for layer in range(depth):
        j = layer // 2
        gpre, gpost = row(mix_pre_g[layer]), row(mix_post_g[layer])
        if layer % 2 == 0:
            h = _even_mixer(h, gpre, gpost, ab_w_in[j], gla_gate_w2[j], gla_gate_b[j], gla_norm_g[j],
                            ab_w_out[j], batch=batch, seq=seq)
        else:
            h = _odd_mixer(h, gpre, gpost, gdn_w_in[j], gdn_conv_w[j], gdn_a_log[j], gdn_dt_bias[j],
                           gdn_norm_g[j], gdn_w_out[j], batch=batch, seq=seq)
        h = xattn_layer(h, row(xattn_pre_g[layer]), xattn_w_q[layer].astype(BF16), kv,
                        xattn_w_o[layer].astype(BF16), row(xattn_post_g[layer]), tm=512, seq=seq)
        h = ffn_layer(h, row(ffn_pre_g[layer]), ffn_w_gate_up[layer].astype(BF16),
                      ffn_w_down[layer].astype(BF16), row(ffn_post_g[layer]), tm=512, tf=512)
    return h.reshape(batch, seq, d)
```

```python
import functools

import jax
import jax.numpy as jnp
from jax import lax
from jax.experimental import pallas as pl
from jax.experimental.pallas import tpu as pltpu

F32 = jnp.float32
BF16 = jnp.bfloat16
RMS_EPS = 1e-6

SB_HEADS = 8
SB_DIM = 128
GLA_HEADS = 4
GLA_DK = 128
GLA_DV = 256
GLA_RANK = 16
GLA_NORMALIZER = 16.0
GDN_KH = 16
GDN_VH = 32
GDN_DIM = 128
GDN_CONV_K = 4
MEM_HEADS = 4
CHUNK = 64
LANES = 128
ROW_CHUNK = 64
SB_LOG_CUTOFF = -104.0


def _cparams(semantics, vmem_mib):
    return pltpu.CompilerParams(dimension_semantics=semantics, vmem_limit_bytes=vmem_mib << 20)


def _rms_rows(x, g):
    ms = jnp.mean(x * x, axis=-1, keepdims=True)
    return x * lax.rsqrt(ms + RMS_EPS) * g


def _softplus(z):
    return jnp.maximum(z, 0.0) + jnp.log1p(jnp.exp(-jnp.abs(z)))


def _silu(x):
    return x * jax.nn.sigmoid(x)


def _split3(x):
    hi = x.astype(BF16)
    r1 = x - hi.astype(F32)
    mid = r1.astype(BF16)
    lo = (r1 - mid.astype(F32)).astype(BF16)
    return hi, mid, lo


def _for_rows(n_rows, body):
    def step(r, c):
        body(pl.ds(pl.multiple_of(r * ROW_CHUNK, ROW_CHUNK), ROW_CHUNK))
        return c
    lax.fori_loop(0, n_rows // ROW_CHUNK, step, 0, unroll=2)


def _prenorm_to(xn_ref, h_ref, g_ref):
    def body(rows):
        xn_ref[rows, :] = _rms_rows(h_ref[rows, :], g_ref[...]).astype(xn_ref.dtype)
    _for_rows(h_ref.shape[0], body)


def _postnorm_residual_to(o_ref, h_ref, acc_ref, g_ref):
    def body(rows):
        o_ref[rows, :] = h_ref[rows, :] + _rms_rows(acc_ref[rows, :], g_ref[...])
    _for_rows(h_ref.shape[0], body)


def _norm_matmul_kernel(h_ref, g_ref, w_ref, o_ref, xn_ref):
    @pl.when(pl.program_id(1) == 0)
    def _():
        _prenorm_to(xn_ref, h_ref, g_ref)
    o_ref[...] = jnp.dot(xn_ref[...], w_ref[...], preferred_element_type=F32).astype(o_ref.dtype)


def norm_matmul(h, g, w, *, tm, tn, out_dtype):
    m, d = h.shape
    n = w.shape[1]
    return pl.pallas_call(
        _norm_matmul_kernel,
        grid=(m // tm, n // tn),
        in_specs=[pl.BlockSpec((tm, d), lambda i, j: (i, 0)),
                  pl.BlockSpec((1, d), lambda i, j: (0, 0)),
                  pl.BlockSpec((d, tn), lambda i, j: (0, j))],
        out_specs=pl.BlockSpec((tm, tn), lambda i, j: (i, j)),
        out_shape=jax.ShapeDtypeStruct((m, n), out_dtype),
        scratch_shapes=[pltpu.VMEM((tm, d), BF16)],
        compiler_params=_cparams(("parallel", "arbitrary"), 48),
        name="norm_matmul",
    )(h, g, w)


def _matmul_post_kernel(*refs):
    a_refs, (w_ref, g_ref, h_ref, o_ref, acc_ref) = refs[:-5], refs[-5:]
    k0 = 0
    for idx, a_ref in enumerate(a_refs):
        kw = a_ref.shape[1]
        part = jnp.dot(a_ref[...], w_ref[k0:k0 + kw, :], preferred_element_type=F32)
        if idx == 0:
            acc_ref[...] = part
        else:
            acc_ref[...] += part
        k0 += kw
    _postnorm_residual_to(o_ref, h_ref, acc_ref, g_ref)


def matmul_post(a_slabs, w, g, h, *, tm):
    m = h.shape[0]
    k, d = w.shape
    return pl.pallas_call(
        _matmul_post_kernel,
        grid=(m // tm,),
        in_specs=[pl.BlockSpec((tm, a.shape[1]), lambda i: (i, 0)) for a in a_slabs] + [
                  pl.BlockSpec((k, d), lambda i: (0, 0), pipeline_mode=pl.Buffered(1)),
                  pl.BlockSpec((1, d), lambda i: (0, 0)),
                  pl.BlockSpec((tm, d), lambda i: (i, 0))],
        out_specs=pl.BlockSpec((tm, d), lambda i: (i, 0)),
        out_shape=jax.ShapeDtypeStruct((m, d), F32),
        scratch_shapes=[pltpu.VMEM((tm, d), F32)],
        compiler_params=_cparams(("parallel",), ((2 * tm * k * 2 + k * d * 2 + 5 * tm * d * 4) >> 20) + 8),
        name="matmul_post",
    )(*a_slabs, w, g, h)


def _ffn_kernel(h_ref, gpre_ref, wg_ref, wu_ref, wd_ref, gpost_ref, o_ref, xn_ref, acc_ref):
    j = pl.program_id(1)

    @pl.when(j == 0)
    def _():
        _prenorm_to(xn_ref, h_ref, gpre_ref)
        acc_ref[...] = jnp.zeros_like(acc_ref)

    xn = xn_ref[...]
    gate = jnp.dot(xn, wg_ref[...], preferred_element_type=F32)
    up = jnp.dot(xn, wu_ref[...], preferred_element_type=F32)
    act = (_silu(gate) * up).astype(BF16)
    acc_ref[...] += jnp.dot(act, wd_ref[...], preferred_element_type=F32)

    @pl.when(j == pl.num_programs(1) - 1)
    def _():
        _postnorm_residual_to(o_ref, h_ref, acc_ref, gpost_ref)


def ffn_layer(h, gpre, w_gate_up, w_down, gpost, *, tm, tf):
    m, d = h.shape
    f = w_down.shape[0]
    nf = f // tf
    return pl.pallas_call(
        _ffn_kernel,
        grid=(m // tm, nf),
        in_specs=[pl.BlockSpec((tm, d), lambda i, j: (i, 0)),
                  pl.BlockSpec((1, d), lambda i, j: (0, 0)),
                  pl.BlockSpec((d, tf), lambda i, j: (0, j)),
                  pl.BlockSpec((d, tf), lambda i, j: (0, j + nf)),
                  pl.BlockSpec((tf, d), lambda i, j: (j, 0)),
                  pl.BlockSpec((1, d), lambda i, j: (0, 0))],
        out_specs=pl.BlockSpec((tm, d), lambda i, j: (i, 0)),
        out_shape=jax.ShapeDtypeStruct((m, d), F32),
        scratch_shapes=[pltpu.VMEM((tm, d), BF16), pltpu.VMEM((tm, d), F32)],
        compiler_params=_cparams(("parallel", "arbitrary"), 48),
        name="ffn_layer",
    )(h, gpre, w_gate_up, w_gate_up, w_down, gpost)


def _xattn_kernel(h_ref, gpre_ref, wq_ref, k_ref, v_ref, wo_ref, gpost_ref, o_ref, xn_ref, att_ref, acc_ref):
    _prenorm_to(xn_ref, h_ref, gpre_ref)
    d = h_ref.shape[1]
    hd = d // MEM_HEADS
    xn = xn_ref[...]
    for i in range(MEM_HEADS):
        cols = slice(i * hd, (i + 1) * hd)
        q = jnp.dot(xn, wq_ref[:, cols], preferred_element_type=F32)
        s = lax.dot_general(q.astype(BF16), k_ref[:, cols], (((1,), (1,)), ((), ())),
                            preferred_element_type=F32) * (hd ** -0.5)
        e = jnp.exp(s - jnp.max(s, axis=-1, keepdims=True))
        p = e / jnp.sum(e, axis=-1, keepdims=True)
        att_ref[:, cols] = jnp.dot(p.astype(BF16), v_ref[:, cols], preferred_element_type=F32).astype(BF16)
    acc_ref[...] = jnp.dot(att_ref[...], wo_ref[...], preferred_element_type=F32)
    _postnorm_residual_to(o_ref, h_ref, acc_ref, gpost_ref)


def xattn_layer(h, gpre, wq, kv, wo, gpost, *, tm, seq):
    m, d = h.shape
    mt = kv.shape[0] // (m // seq)
    per_b = seq // tm
    once = pl.Buffered(1)
    vmem = 4 * tm * d * 4 + 2 * d * d * 2 + 4 * mt * d * 2 + 2 * tm * d * 2 + tm * d * 4
    return pl.pallas_call(
        _xattn_kernel,
        grid=(m // tm,),
        in_specs=[pl.BlockSpec((tm, d), lambda i: (i, 0)),
                  pl.BlockSpec((1, d), lambda i: (0, 0)),
                  pl.BlockSpec((d, d), lambda i: (0, 0), pipeline_mode=once),
                  pl.BlockSpec((mt, d), lambda i: (i // per_b, 0)),
                  pl.BlockSpec((mt, d), lambda i: (i // per_b, 1)),
                  pl.BlockSpec((d, d), lambda i: (0, 0), pipeline_mode=once),
                  pl.BlockSpec((1, d), lambda i: (0, 0))],
        out_specs=pl.BlockSpec((tm, d), lambda i: (i, 0)),
        out_shape=jax.ShapeDtypeStruct((m, d), F32),
        scratch_shapes=[pltpu.VMEM((tm, d), BF16), pltpu.VMEM((tm, d), BF16), pltpu.VMEM((tm, d), F32)],
        compiler_params=_cparams(("parallel",), (vmem >> 20) + 10),
        name="xattn_layer",
    )(h, gpre, wq, kv, kv, wo, gpost)


def _split_heads(x, n, width):
    return jnp.stack([x[:, i * width:(i + 1) * width] for i in range(n)], axis=0)


def _bmm(a, b):
    return lax.dot_general(a, b, (((2,), (1,)), ((0,), (0,))), preferred_element_type=F32)


def _bmm_nt(a, b):
    return lax.dot_general(a, b, (((2,), (2,)), ((0,), (0,))), preferred_element_type=F32)


def _sb_kernel(q_ref, k_ref, v_ref, o_ref, carry_ref, acc_ref):
    tq = q_ref.shape[0]
    nh = SB_HEADS
    qi = pl.program_id(1)
    scale = SB_DIM ** -0.5
    q = _split_heads(q_ref[...], nh, SB_DIM)
    row = lax.broadcasted_iota(jnp.int32, (tq, tq), 0)
    col = lax.broadcasted_iota(jnp.int32, (tq, tq), 1)
    after = jnp.where(row > col, 1.0, 0.0).astype(BF16)

    def block(j, diagonal):
        ks = pl.ds(pl.multiple_of(j * tq, tq), tq)
        kb = _split_heads(k_ref[ks, :], nh, SB_DIM)
        vb = _split_heads(v_ref[ks, :], nh, SB_DIM)
        z = _bmm_nt(q, kb) * scale
        sp = _softplus(z)
        log_keep = -sp
        if diagonal:
            valid = col < row
            log_keep = jnp.where(valid, log_keep, 0.0)
        hi, mid, lo = _split3(log_keep.reshape(nh * tq, tq))
        later = (jnp.dot(hi, after, preferred_element_type=F32)
                 + jnp.dot(mid, after, preferred_element_type=F32)
                 + jnp.dot(lo, after, preferred_element_type=F32)).reshape(nh, tq, tq)
        w = jnp.exp(z - sp + later + carry_ref[...])
        if diagonal:
            w = jnp.where(valid, w, 0.0)
        acc_ref[...] += _bmm(w.astype(BF16), vb)
        carry_ref[...] += jnp.sum(log_keep, axis=-1, keepdims=True)
        return jnp.max(carry_ref[...])

    carry_ref[...] = jnp.zeros_like(carry_ref)
    acc_ref[...] = jnp.zeros_like(acc_ref)
    top = block(qi, True)

    def cond(state):
        j, top = state
        return jnp.logical_and(j >= 0, top > SB_LOG_CUTOFF)

    def body(state):
        j, _ = state
        return j - 1, block(j, False)

    lax.while_loop(cond, body, (qi - 1, top))
    for i in range(nh):
        o_ref[:, i * SB_DIM:(i + 1) * SB_DIM] = acc_ref[i].astype(o_ref.dtype)


def sb_attention(qkv, *, batch, seq, tq):
    m = qkv.shape[0]
    nq = seq // tq
    width = SB_HEADS * SB_DIM
    return pl.pallas_call(
        _sb_kernel,
        grid=(batch, nq),
        in_specs=[pl.BlockSpec((tq, width), lambda b, i: (b * nq + i, 0)),
                  pl.BlockSpec((seq, width), lambda b, i: (b, 1)),
                  pl.BlockSpec((seq, width), lambda b, i: (b, 2))],
        out_specs=pl.BlockSpec((tq, width), lambda b, i: (b * nq + i, 0)),
        out_shape=jax.ShapeDtypeStruct((m, width), BF16),
        scratch_shapes=[pltpu.VMEM((SB_HEADS, tq, 1), F32), pltpu.VMEM((SB_HEADS, tq, SB_DIM), F32)],
        compiler_params=_cparams(("parallel", "arbitrary"), 32),
        name="sb_attention",
    )(qkv, qkv, qkv)


def _chunk_cumsum_matrix(n):
    row = lax.broadcasted_iota(jnp.int32, (n, n), 0)
    col = lax.broadcasted_iota(jnp.int32, (n, n), 1)
    same = (row // CHUNK) == (col // CHUNK)
    return jnp.where(jnp.logical_and(same, col <= row), 1.0, 0.0).astype(BF16)


def _chunk_cumsum(x):
    tri = _chunk_cumsum_matrix(x.shape[0])
    hi, mid, lo = _split3(x)
    return (jnp.dot(tri, hi, preferred_element_type=F32)
            + jnp.dot(tri, mid, preferred_element_type=F32)
            + jnp.dot(tri, lo, preferred_element_type=F32))


def _gla_gate_kernel(h_ref, gpre_ref, w1_ref, w2_ref, b_ref, o_ref):
    xn = _rms_rows(h_ref[...], gpre_ref[...]).astype(BF16)
    low = jnp.dot(xn, w1_ref[...], preferred_element_type=F32)
    pre = jnp.dot(low.astype(BF16), w2_ref[...], preferred_element_type=F32) + b_ref[...]
    log_a = -_softplus(-pre) * (1.0 / GLA_NORMALIZER)
    o_ref[...] = _chunk_cumsum(log_a)


def gla_gate(h, gpre, w1, w2, b, *, tm):
    m, d = h.shape
    n = w2.shape[1]
    return pl.pallas_call(
        _gla_gate_kernel,
        grid=(m // tm,),
        in_specs=[pl.BlockSpec((tm, d), lambda i: (i, 0)),
                  pl.BlockSpec((1, d), lambda i: (0, 0)),
                  pl.BlockSpec((d, LANES), lambda i: (0, 0)),
                  pl.BlockSpec((LANES, n), lambda i: (0, 0)),
                  pl.BlockSpec((1, n), lambda i: (0, 0))],
        out_specs=pl.BlockSpec((tm, n), lambda i: (i, 0)),
        out_shape=jax.ShapeDtypeStruct((m, n), F32),
        compiler_params=_cparams(("parallel",), 32),
        name="gla_gate",
    )(h, gpre, w1, w2, b)


def _gla_kernel(q_ref, k_ref, v_ref, r_ref, gc_ref, ng_ref, o_ref, st_ref):
    @pl.when(pl.program_id(1) == 0)
    def _():
        st_ref[...] = jnp.zeros_like(st_ref)

    tb = q_ref.shape[0]
    nh = GLA_HEADS
    row = lax.broadcasted_iota(jnp.int32, (CHUNK, CHUNK), 0)
    col = lax.broadcasted_iota(jnp.int32, (CHUNK, CHUNK), 1)
    causal = col <= row
    tn = (((1,), (1,)), ((0,), (0,)))
    for c in range(tb // CHUNK):
        rows = slice(c * CHUNK, (c + 1) * CHUNK)
        g = _split_heads(gc_ref[rows, :], nh, GLA_DK)
        g_last = g[:, CHUNK - 1:CHUNK, :]
        k = _split_heads(k_ref[rows, :], nh, GLA_DK)
        v = _split_heads(v_ref[rows, :], nh, GLA_DV).astype(BF16)
        q_dec = (_split_heads(q_ref[rows, :], nh, GLA_DK) * (GLA_DK ** -0.5) * jnp.exp(g)).astype(BF16)
        k_inv = (k * jnp.exp(-g)).astype(BF16)
        k_dec = (k * jnp.exp(g_last - g)).astype(BF16)
        scores = jnp.where(causal, _bmm_nt(q_dec, k_inv), 0.0).astype(BF16)
        st = st_ref[...]
        o = _bmm(scores, v) + _bmm_nt(q_dec, st.astype(BF16))
        st_ref[...] = st * jnp.exp(g_last) + lax.dot_general(v, k_dec, tn, preferred_element_type=F32)
        y = _rms_rows(o, ng_ref[...]) * _silu(_split_heads(r_ref[rows, :], nh, GLA_DV))
        for i in range(nh):
            o_ref[rows, i * GLA_DV:(i + 1) * GLA_DV] = y[i].astype(o_ref.dtype)


def gla_attention(proj, gc, norm_g, *, batch, seq, tb):
    m = proj.shape[0]
    nt = seq // tb
    kw, vw = GLA_HEADS * GLA_DK, GLA_HEADS * GLA_DV
    return pl.pallas_call(
        _gla_kernel,
        grid=(batch, nt),
        in_specs=[pl.BlockSpec((tb, kw), lambda b, t: (b * nt + t, 0)),
                  pl.BlockSpec((tb, kw), lambda b, t: (b * nt + t, 1)),
                  pl.BlockSpec((tb, vw), lambda b, t: (b * nt + t, 2 * kw // vw)),
                  pl.BlockSpec((tb, vw), lambda b, t: (b * nt + t, 2 * kw // vw + 1)),
                  pl.BlockSpec((tb, kw), lambda b, t: (b * nt + t, 0)),
                  pl.BlockSpec((1, GLA_DV), lambda b, t: (0, 0))],
        out_specs=pl.BlockSpec((tb, vw), lambda b, t: (b * nt + t, 0)),
        out_shape=jax.ShapeDtypeStruct((m, vw), BF16),
        scratch_shapes=[pltpu.VMEM((GLA_HEADS, GLA_DV, GLA_DK), F32)],
        compiler_params=_cparams(("parallel", "arbitrary"), 32),
        name="gla_attention",
    )(proj, proj, proj, proj, gc, norm_g)


def _gdn_gate_kernel(h_ref, gpre_ref, w_ref, alog_ref, dtb_ref, o_ref, *, n_heads):
    xn = _rms_rows(h_ref[...], gpre_ref[...]).astype(BF16)
    ba = jnp.dot(xn, w_ref[...], preferred_element_type=F32)
    beta = jax.nn.sigmoid(ba)
    g = -jnp.exp(alog_ref[...]) * _softplus(ba + dtb_ref[...])
    lane = lax.broadcasted_iota(jnp.int32, ba.shape, 1)
    o_ref[...] = jnp.where((lane & (LANES - 1)) < n_heads, beta, _chunk_cumsum(g))


def gdn_gate(h, gpre, w, alog, dtb, *, tm, n_heads):
    m, d = h.shape
    n = w.shape[1]
    return pl.pallas_call(
        functools.partial(_gdn_gate_kernel, n_heads=n_heads),
        grid=(m // tm,),
        in_specs=[pl.BlockSpec((tm, d), lambda i: (i, 0)),
                  pl.BlockSpec((1, d), lambda i: (0, 0)),
                  pl.BlockSpec((d, n), lambda i: (0, 0)),
                  pl.BlockSpec((1, n), lambda i: (0, 0)),
                  pl.BlockSpec((1, n), lambda i: (0, 0))],
        out_specs=pl.BlockSpec((tm, n), lambda i: (i, 0)),
        out_shape=jax.ShapeDtypeStruct((m, n), F32),
        compiler_params=_cparams(("parallel",), 32),
        name="gdn_gate",
    )(h, gpre, w, alog, dtb)


def _gdn_conv_kernel(x_ref, prev_ref, w_ref, o_ref, xs_ref, *, l2_normalize):
    tb, tc = x_ref.shape
    halo = 8
    first = pl.program_id(1) == 0
    xs_ref[0:halo, :] = jnp.where(first, 0.0, prev_ref[...])
    xs_ref[halo:halo + tb, :] = x_ref[...]
    for r in range(tb // ROW_CHUNK):
        for hh in range(tc // GDN_DIM):
            cols = slice(hh * GDN_DIM, (hh + 1) * GDN_DIM)
            y = jnp.zeros((ROW_CHUNK, GDN_DIM), F32)
            for i in range(GDN_CONV_K):
                off = r * ROW_CHUNK + halo - (GDN_CONV_K - 1) + i
                y = y + w_ref[i:i + 1, cols] * xs_ref[off:off + ROW_CHUNK, cols]
            y = _silu(y)
            if l2_normalize:
                y = y * lax.rsqrt(jnp.sum(y * y, axis=-1, keepdims=True) + RMS_EPS)
            o_ref[r * ROW_CHUNK:(r + 1) * ROW_CHUNK, cols] = y.astype(o_ref.dtype)


def gdn_conv(proj, conv_w, *, col0, width, l2_normalize, batch, seq, tb, tc):
    m = proj.shape[0]
    nt = seq // tb
    halo_blocks = tb // 8
    c0 = col0 // tc
    return pl.pallas_call(
        functools.partial(_gdn_conv_kernel, l2_normalize=l2_normalize),
        grid=(batch, nt, width // tc),
        in_specs=[pl.BlockSpec((tb, tc), lambda b, t, c: (b * nt + t, c0 + c)),
                  pl.BlockSpec((8, tc), lambda b, t, c: (jnp.maximum((b * nt + t) * halo_blocks - 1, 0), c0 + c)),
                  pl.BlockSpec((GDN_CONV_K, tc), lambda b, t, c: (0, c0 + c))],
        out_specs=pl.BlockSpec((tb, tc), lambda b, t, c: (b * nt + t, c)),
        out_shape=jax.ShapeDtypeStruct((m, width), F32),
        scratch_shapes=[pltpu.VMEM((tb + 8, tc), F32)],
        compiler_params=_cparams(("parallel", "parallel", "parallel"), 32),
        name="gdn_conv",
    )(proj, proj, conv_w)


def _gdn_kernel(q_ref, k_ref, v_ref, z_ref, col_ref, grow_ref, glast_ref, ng_ref, o_ref,
                s_ref, u_ref, w_ref, qkd_ref, q16_ref, k16_ref, eg_ref, ed_ref, egl_ref):
    @pl.when(pl.program_id(2) == 0)
    def _():
        s_ref[...] = jnp.zeros_like(s_ref)

    groups, nc = u_ref.shape[0], u_ref.shape[1]
    n = groups * nc
    d = GDN_DIM
    c2 = 2 * CHUNK
    row = lax.broadcasted_iota(jnp.int32, (CHUNK, c2), 0)
    lane = lax.broadcasted_iota(jnp.int32, (CHUNK, c2), 1)
    first_head = lane < CHUNK
    key_pos = jnp.where(first_head, lane, lane - CHUNK)
    incl = key_pos <= row
    strict = key_pos < row
    diag = key_pos == row

    def chunked(ref, width):
        return jnp.concatenate([ref[:, g * width:(g + 1) * width].reshape(nc, CHUNK, width)
                                for g in range(groups)], axis=0)

    def block_diag(m16):
        zero = jnp.zeros_like(m16)
        return jnp.concatenate([jnp.where(first_head, m16, zero), jnp.where(first_head, zero, m16)], axis=1)

    q = chunked(q_ref, d).astype(F32) * (d ** -0.5)
    k = chunked(k_ref, d).astype(F32)
    v = chunked(v_ref, 2 * d).astype(F32)
    cols = col_ref[...].reshape(nc, CHUNK, LANES)
    nv = 2 * groups

    def token_column(lane0):
        return jnp.concatenate([cols[:, :, lane0 + 2 * g:lane0 + 2 * g + 1] for g in range(groups)], axis=0)

    beta_a, beta_b = token_column(0), token_column(1)
    g_a, g_b = token_column(nv), token_column(nv + 1)
    g_row = jnp.concatenate([grow_ref[:, g:g + 1, :] for g in range(groups)], axis=0)
    g_end = jnp.concatenate([glast_ref[:, g:g + 1, :] for g in range(groups)], axis=0)

    k16 = k.astype(BF16)
    q16 = q.astype(BF16)
    prod = _bmm_nt(jnp.concatenate([k16, q16], axis=1), jnp.concatenate([k16, k16], axis=1))
    kk2, qk2 = prod[:, :CHUNK], prod[:, CHUNK:]

    g_col2 = jnp.where(first_head, g_a, g_b)
    beta2 = jnp.where(first_head, beta_a, beta_b)
    decay2 = jnp.where(incl, jnp.exp(jnp.where(incl, g_col2 - g_row, 0.0)), 0.0)
    x = jnp.where(strict, -(beta2 * kk2 * decay2), 0.0)
    qkd_ref[...] = (qk2 * decay2).astype(BF16).reshape(qkd_ref.shape)

    r = jnp.zeros_like(x)
    p = jnp.where((row >> 3) == (key_pos >> 3), x, 0.0)
    for _ in range(3):
        p16 = p.astype(BF16)
        y = _bmm(jnp.concatenate([r.astype(BF16), p16], axis=1), block_diag(p16))
        r = r + p + y[:, :CHUNK]
        p = y[:, CHUNK:]
    tinv = r + jnp.where(diag, 1.0, 0.0)
    for shift in (3, 4, 5):
        couples = jnp.logical_and((row >> (shift + 1)) == (key_pos >> (shift + 1)),
                                  (row >> shift) != (key_pos >> shift))
        xc16 = jnp.where(couples, x, 0.0).astype(BF16)
        t16 = tinv.astype(BF16)
        y = _bmm(xc16, block_diag(t16))
        tinv = tinv + _bmm(t16, block_diag(y.astype(BF16)))
    r = jnp.where(diag, 0.0, tinv)

    eg_a, eg_b = jnp.exp(g_a), jnp.exp(g_b)
    vb_a, vb_b = v[:, :, :d] * beta_a, v[:, :, d:] * beta_b
    kg_a, kg_b = k * (beta_a * eg_a), k * (beta_b * eg_b)
    rhs2 = jnp.concatenate([jnp.concatenate([vb_a, kg_a], axis=2),
                            jnp.concatenate([vb_b, kg_b], axis=2)], axis=1).astype(BF16)
    corr = _bmm(block_diag(r.astype(BF16)), rhs2)
    u_ref[...] = jnp.concatenate([vb_a + corr[:, :CHUNK, :d], vb_b + corr[:, CHUNK:, :d]],
                                 axis=2).reshape(u_ref.shape)
    w_ref[...] = jnp.concatenate([kg_a + corr[:, :CHUNK, d:], kg_b + corr[:, CHUNK:, d:]],
                                 axis=1).astype(BF16).reshape(w_ref.shape)
    q16_ref[...] = q16.reshape(q16_ref.shape)
    k16_ref[...] = k16.reshape(k16_ref.shape)
    lanes_d = (n, CHUNK, d)
    eg_ref[...] = jnp.concatenate([jnp.broadcast_to(eg_a, lanes_d), jnp.broadcast_to(eg_b, lanes_d)],
                                  axis=2).reshape(eg_ref.shape)
    ed_ref[...] = jnp.concatenate([jnp.exp(g_end[:, :, :d] - g_a), jnp.exp(g_end[:, :, d:] - g_b)],
                                  axis=2).reshape(ed_ref.shape)
    egl_ref[...] = jnp.exp(g_end).reshape(egl_ref.shape)

    zeros_vd = jnp.zeros((groups, CHUNK, d), BF16)
    tn = (((1,), (1,)), ((0,), (0,)))
    for c in range(nc):
        rows = slice(c * CHUNK, (c + 1) * CHUNK)
        s = s_ref[...]
        s16 = s.astype(BF16)
        ws = _bmm(jnp.concatenate([w_ref[:, c], q16_ref[:, c]], axis=1), s16)
        vn = u_ref[:, c] - jnp.concatenate([ws[:, :CHUNK, :d], ws[:, CHUNK:c2, d:]], axis=2)
        vn16 = vn.astype(BF16)
        vn_bd = jnp.concatenate([jnp.concatenate([vn16[:, :, :d], zeros_vd], axis=2),
                                 jnp.concatenate([zeros_vd, vn16[:, :, d:]], axis=2)], axis=1)
        o = eg_ref[:, c] * ws[:, c2:] + _bmm(qkd_ref[:, c], vn_bd)
        vn_dec = (vn * ed_ref[:, c]).astype(BF16)
        s_ref[...] = s * egl_ref[:, c] + lax.dot_general(k16_ref[:, c], vn_dec, tn,
                                                         preferred_element_type=F32)
        for g in range(groups):
            for e in range(2):
                cs = slice((2 * g + e) * d, (2 * g + e + 1) * d)
                o_ref[rows, cs] = (_rms_rows(o[g, :, e * d:(e + 1) * d], ng_ref[...])
                                   * _silu(z_ref[rows, cs])).astype(o_ref.dtype)


def gdn_attention(qk, vv, proj, colv, grow, glast, norm_g, *, batch, seq, tb, groups):
    m = qk.shape[0]
    nt = seq // tb
    nc = tb // CHUNK
    d = GDN_DIM
    kw, vw = groups * d, groups * 2 * d
    k0 = GDN_KH * d // kw
    z0 = (2 * GDN_KH + GDN_VH) * d // vw
    per = lambda *shape: (groups, nc) + shape
    return pl.pallas_call(
        _gdn_kernel,
        grid=(batch, GDN_KH // groups, nt),
        in_specs=[pl.BlockSpec((tb, kw), lambda b, h, t: (b * nt + t, h)),
                  pl.BlockSpec((tb, kw), lambda b, h, t: (b * nt + t, k0 + h)),
                  pl.BlockSpec((tb, vw), lambda b, h, t: (b * nt + t, h)),
                  pl.BlockSpec((tb, vw), lambda b, h, t: (b * nt + t, z0 + h)),
                  pl.BlockSpec((tb, LANES), lambda b, h, t: (b * nt + t, h)),
                  pl.BlockSpec((None, None, nc, 8, 2 * CHUNK), lambda b, h, t: (b, h, t, 0, 0)),
                  pl.BlockSpec((None, None, nc, 8, 2 * d), lambda b, h, t: (b, h, t, 0, 0)),
                  pl.BlockSpec((1, d), lambda b, h, t: (0, 0))],
        out_specs=pl.BlockSpec((tb, vw), lambda b, h, t: (b * nt + t, h)),
        out_shape=jax.ShapeDtypeStruct((m, GDN_VH * d), BF16),
        scratch_shapes=[pltpu.VMEM((groups, d, 2 * d), F32),
                        pltpu.VMEM(per(CHUNK, 2 * d), F32),
                        pltpu.VMEM(per(2 * CHUNK, d), BF16),
                        pltpu.VMEM(per(CHUNK, 2 * CHUNK), BF16),
                        pltpu.VMEM(per(CHUNK, d), BF16),
                        pltpu.VMEM(per(CHUNK, d), BF16),
                        pltpu.VMEM(per(CHUNK, 2 * d), F32),
                        pltpu.VMEM(per(CHUNK, 2 * d), F32),
                        pltpu.VMEM(per(1, 2 * d), F32)],
        compiler_params=_cparams(("parallel", "parallel", "arbitrary"), 32),
        name="gdn_attention",
    )(qk, qk, vv, proj, colv, grow, glast, norm_g)


def _pad_cols(w, n):
    return jnp.pad(w, ((0, 0), (0, n - w.shape[1])))


def _even_mixer(h, gpre, gpost, w_in, gate_w2, gate_b, gla_norm_g, w_out, *, batch, seq):
    n_sb = 3 * SB_HEADS * SB_DIM
    n_main = w_in.shape[1] - GLA_RANK
    sb_qkv = norm_matmul(h, gpre, w_in[:, :n_sb].astype(BF16), tm=1024, tn=1024, out_dtype=BF16)
    gla_proj = norm_matmul(h, gpre, w_in[:, n_sb:n_main].astype(BF16), tm=1024, tn=1024, out_dtype=F32)
    w1 = _pad_cols(w_in[:, n_main:], LANES).astype(BF16)
    w2 = jnp.pad(gate_w2, ((0, LANES - GLA_RANK), (0, 0))).astype(BF16)
    gc = gla_gate(h, gpre, w1, w2, gate_b[None, :], tm=256)
    o_sb = sb_attention(sb_qkv, batch=batch, seq=seq, tq=128)
    o_gla = gla_attention(gla_proj, gc, gla_norm_g[None, :], batch=batch, seq=seq, tb=256)
    return matmul_post((o_sb, o_gla), w_out.astype(BF16), gpost, h, tm=512)


def _odd_mixer(h, gpre, gpost, w_in, conv_w, a_log, dt_bias, norm_g, w_out, *, batch, seq):
    n_main = w_in.shape[1] - 2 * GDN_VH
    conv_dim = conv_w.shape[1]
    proj = norm_matmul(h, gpre, w_in[:, :n_main].astype(BF16), tm=1024, tn=1024, out_dtype=F32)
    groups = 4
    ng, nv = GDN_KH // groups, 2 * groups
    d_model = w_in.shape[0]
    slab = lambda first, second, lead: jnp.concatenate(
        [first.reshape(lead + (ng, nv)), second.reshape(lead + (ng, nv)),
         jnp.zeros(lead + (ng, LANES - 2 * nv), F32)], axis=-1).reshape(lead + (ng * LANES,))
    w_ba = slab(w_in[:, n_main:n_main + GDN_VH], w_in[:, n_main + GDN_VH:], (d_model,)).astype(BF16)
    zeros_vh = jnp.zeros((GDN_VH,), F32)
    gates = gdn_gate(h, gpre, w_ba, slab(zeros_vh, a_log, ())[None, :], slab(zeros_vh, dt_bias, ())[None, :],
                     tm=256, n_heads=nv)
    qk_width = 2 * GDN_KH * GDN_DIM
    qk = gdn_conv(proj, conv_w, col0=0, width=qk_width, l2_normalize=True, batch=batch, seq=seq, tb=512, tc=1024)
    vv = gdn_conv(proj, conv_w, col0=qk_width, width=conv_dim - qk_width, l2_normalize=False,
                  batch=batch, seq=seq, tb=512, tc=1024)

    nc = seq // CHUNK
    gcum = gates.reshape(batch, nc, CHUNK, ng, LANES)[..., nv:2 * nv].reshape(batch, nc, CHUNK, ng, groups, 2)
    pad_sublanes = lambda a: jnp.pad(a, ((0, 0), (0, 0), (0, 0), (0, 8 - groups), (0, 0)))
    grow = pad_sublanes(gcum.transpose(0, 3, 1, 4, 5, 2).reshape(batch, ng, nc, groups, 2 * CHUNK))
    glast = pad_sublanes(jnp.repeat(gcum[:, :, CHUNK - 1].transpose(0, 2, 1, 3, 4), GDN_DIM, axis=-1))

    o = gdn_attention(qk, vv, proj, gates, grow, glast, norm_g[None, :], batch=batch, seq=seq, tb=256,
                      groups=groups)
    return matmul_post((o,), w_out.astype(BF16), gpost, h, tm=512)


def kernel(x, mem, mem_norm_g, mem_w_kv, mix_pre_g, mix_post_g, ab_w_in, gla_gate_w2, gla_gate_b, gla_norm_g, ab_w_out, gdn_w_in, gdn_conv_w, gdn_a_log, gdn_dt_bias, gdn_norm_g, gdn_w_out, xattn_pre_g, xattn_post_g, xattn_w_q, xattn_w_o, ffn_pre_g, ffn_post_g, ffn_w_gate_up, ffn_w_down):
    batch, seq, d = x.shape
    depth = mix_pre_g.shape[0]
    mem_tokens = mem.shape[1]
    row = lambda g: g[None, :]

    kv = norm_matmul(mem.reshape(batch * mem_tokens, d), row(mem_norm_g), mem_w_kv.astype(BF16),
                     tm=batch * mem_tokens, tn=1024, out_dtype=BF16)

    h = x.reshape(batch * seq, d)
    for layer in range(depth):
        j = layer // 2
        gpre, gpost = row(mix_pre_g[layer]), row(mix_post_g[layer])
        if layer % 2 == 0:
            h = _even_mixer(h, gpre, gpost, ab_w_in[j], gla_gate_w2[j], gla_gate_b[j], gla_norm_g[j],
                            ab_w_out[j], batch=batch, seq=seq)
        else:
            h = _odd_mixer(h, gpre, gpost, gdn_w_in[j], gdn_conv_w[j], gdn_a_log[j], gdn_dt_bias[j],
                           gdn_norm_g[j], gdn_w_out[j], batch=batch, seq=seq)
        h = xattn_layer(h, row(xattn_pre_g[layer]), xattn_w_q[layer].astype(BF16), kv,
                        xattn_w_o[layer].astype(BF16), row(xattn_post_g[layer]), tm=512, seq=seq)
        h = ffn_layer(h, row(ffn_pre_g[layer]), ffn_w_gate_up[layer].astype(BF16),
                      ffn_w_down[layer].astype(BF16), row(ffn_post_g[layer]), tm=512, tf=512)
    return h.reshape(batch, seq, d)
```

```python
import functools

import jax
import jax.numpy as jnp
from jax import lax
from jax.experimental import pallas as pl
from jax.experimental.pallas import tpu as pltpu

F32 = jnp.float32
BF16 = jnp.bfloat16
RMS_EPS = 1e-6

SB_HEADS = 8
SB_DIM = 128
GLA_HEADS = 4
GLA_DK = 128
GLA_DV = 256
GLA_RANK = 16
GLA_NORMALIZER = 16.0
GDN_KH = 16
GDN_VH = 32
GDN_DIM = 128
GDN_CONV_K = 4
MEM_HEADS = 4
CHUNK = 64
LANES = 128
ROW_CHUNK = 64
SB_LOG_CUTOFF = -104.0


def _cparams(semantics, vmem_mib):
    return pltpu.CompilerParams(dimension_semantics=semantics, vmem_limit_bytes=vmem_mib << 20)


def _rms_rows(x, g):
    ms = jnp.mean(x * x, axis=-1, keepdims=True)
    return x * lax.rsqrt(ms + RMS_EPS) * g


def _softplus(z):
    return jnp.maximum(z, 0.0) + jnp.log1p(jnp.exp(-jnp.abs(z)))


def _silu(x):
    half = 0.5 * x
    return half + half * jnp.tanh(half)


def _split3(x):
    hi = x.astype(BF16)
    r1 = x - hi.astype(F32)
    mid = r1.astype(BF16)
    lo = (r1 - mid.astype(F32)).astype(BF16)
    return hi, mid, lo


def _for_rows(n_rows, body):
    def step(r, c):
        body(pl.ds(pl.multiple_of(r * ROW_CHUNK, ROW_CHUNK), ROW_CHUNK))
        return c
    lax.fori_loop(0, n_rows // ROW_CHUNK, step, 0, unroll=2)


def _prenorm_to(xn_ref, h_ref, g_ref):
    def body(rows):
        xn_ref[rows, :] = _rms_rows(h_ref[rows, :], g_ref[...]).astype(xn_ref.dtype)
    _for_rows(h_ref.shape[0], body)


def _postnorm_residual_to(o_ref, h_ref, acc_ref, g_ref):
    def body(rows):
        o_ref[rows, :] = h_ref[rows, :] + _rms_rows(acc_ref[rows, :], g_ref[...])
    _for_rows(h_ref.shape[0], body)


def _norm_matmul_kernel(h_ref, g_ref, w_ref, o_ref, xn_ref):
    @pl.when(pl.program_id(1) == 0)
    def _():
        _prenorm_to(xn_ref, h_ref, g_ref)
    o_ref[...] = jnp.dot(xn_ref[...], w_ref[...], preferred_element_type=F32).astype(o_ref.dtype)


def norm_matmul(h, g, w, *, tm, tn, out_dtype):
    m, d = h.shape
    n = w.shape[1]
    return pl.pallas_call(
        _norm_matmul_kernel,
        grid=(m // tm, n // tn),
        in_specs=[pl.BlockSpec((tm, d), lambda i, j: (i, 0)),
                  pl.BlockSpec((1, d), lambda i, j: (0, 0)),
                  pl.BlockSpec((d, tn), lambda i, j: (0, j))],
        out_specs=pl.BlockSpec((tm, tn), lambda i, j: (i, j)),
        out_shape=jax.ShapeDtypeStruct((m, n), out_dtype),
        scratch_shapes=[pltpu.VMEM((tm, d), BF16)],
        compiler_params=_cparams(("parallel", "arbitrary"), 48),
        name="norm_matmul",
    )(h, g, w)


def _matmul_post_kernel(*refs):
    a_refs, (w_ref, g_ref, h_ref, o_ref, acc_ref) = refs[:-5], refs[-5:]
    k0 = 0
    for idx, a_ref in enumerate(a_refs):
        kw = a_ref.shape[1]
        part = jnp.dot(a_ref[...], w_ref[k0:k0 + kw, :], preferred_element_type=F32)
        if idx == 0:
            acc_ref[...] = part
        else:
            acc_ref[...] += part
        k0 += kw
    _postnorm_residual_to(o_ref, h_ref, acc_ref, g_ref)


def matmul_post(a_slabs, w, g, h, *, tm):
    m = h.shape[0]
    k, d = w.shape
    return pl.pallas_call(
        _matmul_post_kernel,
        grid=(m // tm,),
        in_specs=[pl.BlockSpec((tm, a.shape[1]), lambda i: (i, 0)) for a in a_slabs] + [
                  pl.BlockSpec((k, d), lambda i: (0, 0), pipeline_mode=pl.Buffered(1)),
                  pl.BlockSpec((1, d), lambda i: (0, 0)),
                  pl.BlockSpec((tm, d), lambda i: (i, 0))],
        out_specs=pl.BlockSpec((tm, d), lambda i: (i, 0)),
        out_shape=jax.ShapeDtypeStruct((m, d), F32),
        scratch_shapes=[pltpu.VMEM((tm, d), F32)],
        compiler_params=_cparams(("parallel",), ((2 * tm * k * 2 + k * d * 2 + 5 * tm * d * 4) >> 20) + 8),
        name="matmul_post",
    )(*a_slabs, w, g, h)


def _ffn_kernel(h_ref, gpre_ref, wg_ref, wu_ref, wd_ref, gpost_ref, o_ref, xn_ref, acc_ref):
    j = pl.program_id(1)

    @pl.when(j == 0)
    def _():
        _prenorm_to(xn_ref, h_ref, gpre_ref)
        acc_ref[...] = jnp.zeros_like(acc_ref)

    xn = xn_ref[...]
    gate = jnp.dot(xn, wg_ref[...], preferred_element_type=F32)
    up = jnp.dot(xn, wu_ref[...], preferred_element_type=F32)
    act = (_silu(gate) * up).astype(BF16)
    acc_ref[...] += jnp.dot(act, wd_ref[...], preferred_element_type=F32)

    @pl.when(j == pl.num_programs(1) - 1)
    def _():
        _postnorm_residual_to(o_ref, h_ref, acc_ref, gpost_ref)


def ffn_layer(h, gpre, w_gate_up, w_down, gpost, *, tm, tf):
    m, d = h.shape
    f = w_down.shape[0]
    nf = f // tf
    return pl.pallas_call(
        _ffn_kernel,
        grid=(m // tm, nf),
        in_specs=[pl.BlockSpec((tm, d), lambda i, j: (i, 0)),
                  pl.BlockSpec((1, d), lambda i, j: (0, 0)),
                  pl.BlockSpec((d, tf), lambda i, j: (0, j)),
                  pl.BlockSpec((d, tf), lambda i, j: (0, j + nf)),
                  pl.BlockSpec((tf, d), lambda i, j: (j, 0)),
                  pl.BlockSpec((1, d), lambda i, j: (0, 0))],
        out_specs=pl.BlockSpec((tm, d), lambda i, j: (i, 0)),
        out_shape=jax.ShapeDtypeStruct((m, d), F32),
        scratch_shapes=[pltpu.VMEM((tm, d), BF16), pltpu.VMEM((tm, d), F32)],
        compiler_params=_cparams(("parallel", "arbitrary"), 48),
        name="ffn_layer",
    )(h, gpre, w_gate_up, w_gate_up, w_down, gpost)


def _xattn_kernel(h_ref, gpre_ref, wq_ref, k_ref, v_ref, wo_ref, gpost_ref, o_ref, xn_ref, att_ref, acc_ref):
    _prenorm_to(xn_ref, h_ref, gpre_ref)
    d = h_ref.shape[1]
    hd = d // MEM_HEADS
    xn = xn_ref[...]
    for i in range(MEM_HEADS):
        cols = slice(i * hd, (i + 1) * hd)
        q = jnp.dot(xn, wq_ref[:, cols], preferred_element_type=F32)
        s = lax.dot_general(q.astype(BF16), k_ref[:, cols], (((1,), (1,)), ((), ())),
                            preferred_element_type=F32) * (hd ** -0.5)
        e = jnp.exp(s - jnp.max(s, axis=-1, keepdims=True))
        p = e / jnp.sum(e, axis=-1, keepdims=True)
        att_ref[:, cols] = jnp.dot(p.astype(BF16), v_ref[:, cols], preferred_element_type=F32).astype(BF16)
    acc_ref[...] = jnp.dot(att_ref[...], wo_ref[...], preferred_element_type=F32)
    _postnorm_residual_to(o_ref, h_ref, acc_ref, gpost_ref)


def xattn_layer(h, gpre, wq, kv, wo, gpost, *, tm, seq):
    m, d = h.shape
    mt = kv.shape[0] // (m // seq)
    per_b = seq // tm
    once = pl.Buffered(1)
    vmem = 4 * tm * d * 4 + 2 * d * d * 2 + 4 * mt * d * 2 + 2 * tm * d * 2 + tm * d * 4
    return pl.pallas_call(
        _xattn_kernel,
        grid=(m // tm,),
        in_specs=[pl.BlockSpec((tm, d), lambda i: (i, 0)),
                  pl.BlockSpec((1, d), lambda i: (0, 0)),
                  pl.BlockSpec((d, d), lambda i: (0, 0), pipeline_mode=once),
                  pl.BlockSpec((mt, d), lambda i: (i // per_b, 0)),
                  pl.BlockSpec((mt, d), lambda i: (i // per_b, 1)),
                  pl.BlockSpec((d, d), lambda i: (0, 0), pipeline_mode=once),
                  pl.BlockSpec((1, d), lambda i: (0, 0))],
        out_specs=pl.BlockSpec((tm, d), lambda i: (i, 0)),
        out_shape=jax.ShapeDtypeStruct((m, d), F32),
        scratch_shapes=[pltpu.VMEM((tm, d), BF16), pltpu.VMEM((tm, d), BF16), pltpu.VMEM((tm, d), F32)],
        compiler_params=_cparams(("parallel",), (vmem >> 20) + 10),
        name="xattn_layer",
    )(h, gpre, wq, kv, kv, wo, gpost)


def _split_heads(x, n, width):
    return jnp.stack([x[:, i * width:(i + 1) * width] for i in range(n)], axis=0)


def _bmm(a, b):
    return lax.dot_general(a, b, (((2,), (1,)), ((0,), (0,))), preferred_element_type=F32)


def _bmm_nt(a, b):
    return lax.dot_general(a, b, (((2,), (2,)), ((0,), (0,))), preferred_element_type=F32)


def _sb_kernel(q_ref, k_ref, v_ref, o_ref, carry_ref, acc_ref):
    tq = q_ref.shape[0]
    nh = SB_HEADS
    qi = pl.program_id(1)
    scale = SB_DIM ** -0.5
    q = _split_heads(q_ref[...], nh, SB_DIM)
    row = lax.broadcasted_iota(jnp.int32, (tq, tq), 0)
    col = lax.broadcasted_iota(jnp.int32, (tq, tq), 1)
    after = jnp.where(row > col, 1.0, 0.0).astype(BF16)

    def block(j, diagonal):
        ks = pl.ds(pl.multiple_of(j * tq, tq), tq)
        kb = _split_heads(k_ref[ks, :], nh, SB_DIM)
        vb = _split_heads(v_ref[ks, :], nh, SB_DIM)
        z = _bmm_nt(q, kb) * scale
        sp = _softplus(z)
        log_keep = -sp
        if diagonal:
            valid = col < row
            log_keep = jnp.where(valid, log_keep, 0.0)
        hi, mid, lo = _split3(log_keep.reshape(nh * tq, tq))
        later = (jnp.dot(hi, after, preferred_element_type=F32)
                 + jnp.dot(mid, after, preferred_element_type=F32)
                 + jnp.dot(lo, after, preferred_element_type=F32)).reshape(nh, tq, tq)
        w = jnp.exp(z - sp + later + carry_ref[...])
        if diagonal:
            w = jnp.where(valid, w, 0.0)
        acc_ref[...] += _bmm(w.astype(BF16), vb)
        carry_ref[...] += jnp.sum(log_keep, axis=-1, keepdims=True)
        return jnp.max(carry_ref[...])

    carry_ref[...] = jnp.zeros_like(carry_ref)
    acc_ref[...] = jnp.zeros_like(acc_ref)
    top = block(qi, True)

    def cond(state):
        j, top = state
        return jnp.logical_and(j >= 0, top > SB_LOG_CUTOFF)

    def body(state):
        j, _ = state
        return j - 1, block(j, False)

    lax.while_loop(cond, body, (qi - 1, top))
    for i in range(nh):
        o_ref[:, i * SB_DIM:(i + 1) * SB_DIM] = acc_ref[i].astype(o_ref.dtype)


def sb_attention(qkv, *, batch, seq, tq):
    m = qkv.shape[0]
    nq = seq // tq
    width = SB_HEADS * SB_DIM
    return pl.pallas_call(
        _sb_kernel,
        grid=(batch, nq),
        in_specs=[pl.BlockSpec((tq, width), lambda b, i: (b * nq + i, 0)),
                  pl.BlockSpec((seq, width), lambda b, i: (b, 1)),
                  pl.BlockSpec((seq, width), lambda b, i: (b, 2))],
        out_specs=pl.BlockSpec((tq, width), lambda b, i: (b * nq + i, 0)),
        out_shape=jax.ShapeDtypeStruct((m, width), BF16),
        scratch_shapes=[pltpu.VMEM((SB_HEADS, tq, 1), F32), pltpu.VMEM((SB_HEADS, tq, SB_DIM), F32)],
        compiler_params=_cparams(("parallel", "arbitrary"), 32),
        name="sb_attention",
    )(qkv, qkv, qkv)


def _chunk_cumsum_matrix(n):
    row = lax.broadcasted_iota(jnp.int32, (n, n), 0)
    col = lax.broadcasted_iota(jnp.int32, (n, n), 1)
    same = (row // CHUNK) == (col // CHUNK)
    return jnp.where(jnp.logical_and(same, col <= row), 1.0, 0.0).astype(BF16)


def _chunk_cumsum(x):
    tri = _chunk_cumsum_matrix(x.shape[0])
    hi, mid, lo = _split3(x)
    return (jnp.dot(tri, hi, preferred_element_type=F32)
            + jnp.dot(tri, mid, preferred_element_type=F32)
            + jnp.dot(tri, lo, preferred_element_type=F32))


def _gla_gate_kernel(h_ref, gpre_ref, w1_ref, w2_ref, b_ref, o_ref):
    xn = _rms_rows(h_ref[...], gpre_ref[...]).astype(BF16)
    low = jnp.dot(xn, w1_ref[...], preferred_element_type=F32)
    pre = jnp.dot(low.astype(BF16), w2_ref[...], preferred_element_type=F32) + b_ref[...]
    log_a = -_softplus(-pre) * (1.0 / GLA_NORMALIZER)
    o_ref[...] = _chunk_cumsum(log_a)


def gla_gate(h, gpre, w1, w2, b, *, tm):
    m, d = h.shape
    n = w2.shape[1]
    return pl.pallas_call(
        _gla_gate_kernel,
        grid=(m // tm,),
        in_specs=[pl.BlockSpec((tm, d), lambda i: (i, 0)),
                  pl.BlockSpec((1, d), lambda i: (0, 0)),
                  pl.BlockSpec((d, LANES), lambda i: (0, 0)),
                  pl.BlockSpec((LANES, n), lambda i: (0, 0)),
                  pl.BlockSpec((1, n), lambda i: (0, 0))],
        out_specs=pl.BlockSpec((tm, n), lambda i: (i, 0)),
        out_shape=jax.ShapeDtypeStruct((m, n), F32),
        compiler_params=_cparams(("parallel",), 32),
        name="gla_gate",
    )(h, gpre, w1, w2, b)


def _gla_kernel(q_ref, k_ref, v_ref, r_ref, gc_ref, ng_ref, o_ref, st_ref):
    @pl.when(pl.program_id(1) == 0)
    def _():
        st_ref[...] = jnp.zeros_like(st_ref)

    tb = q_ref.shape[0]
    nh = GLA_HEADS
    row = lax.broadcasted_iota(jnp.int32, (CHUNK, CHUNK), 0)
    col = lax.broadcasted_iota(jnp.int32, (CHUNK, CHUNK), 1)
    causal = col <= row
    tn = (((1,), (1,)), ((0,), (0,)))
    for c in range(tb // CHUNK):
        rows = slice(c * CHUNK, (c + 1) * CHUNK)
        g = _split_heads(gc_ref[rows, :], nh, GLA_DK)
        g_last = g[:, CHUNK - 1:CHUNK, :]
        k = _split_heads(k_ref[rows, :], nh, GLA_DK)
        v = _split_heads(v_ref[rows, :], nh, GLA_DV).astype(BF16)
        q_dec = (_split_heads(q_ref[rows, :], nh, GLA_DK) * (GLA_DK ** -0.5) * jnp.exp(g)).astype(BF16)
        k_inv = (k * jnp.exp(-g)).astype(BF16)
        k_dec = (k * jnp.exp(g_last - g)).astype(BF16)
        scores = jnp.where(causal, _bmm_nt(q_dec, k_inv), 0.0).astype(BF16)
        st = st_ref[...]
        o = _bmm(scores, v) + _bmm_nt(q_dec, st.astype(BF16))
        st_ref[...] = st * jnp.exp(g_last) + lax.dot_general(v, k_dec, tn, preferred_element_type=F32)
        y = _rms_rows(o, ng_ref[...]) * _silu(_split_heads(r_ref[rows, :], nh, GLA_DV))
        for i in range(nh):
            o_ref[rows, i * GLA_DV:(i + 1) * GLA_DV] = y[i].astype(o_ref.dtype)


def gla_attention(proj, gc, norm_g, *, batch, seq, tb):
    m = proj.shape[0]
    nt = seq // tb
    kw, vw = GLA_HEADS * GLA_DK, GLA_HEADS * GLA_DV
    return pl.pallas_call(
        _gla_kernel,
        grid=(batch, nt),
        in_specs=[pl.BlockSpec((tb, kw), lambda b, t: (b * nt + t, 0)),
                  pl.BlockSpec((tb, kw), lambda b, t: (b * nt + t, 1)),
                  pl.BlockSpec((tb, vw), lambda b, t: (b * nt + t, 2 * kw // vw)),
                  pl.BlockSpec((tb, vw), lambda b, t: (b * nt + t, 2 * kw // vw + 1)),
                  pl.BlockSpec((tb, kw), lambda b, t: (b * nt + t, 0)),
                  pl.BlockSpec((1, GLA_DV), lambda b, t: (0, 0))],
        out_specs=pl.BlockSpec((tb, vw), lambda b, t: (b * nt + t, 0)),
        out_shape=jax.ShapeDtypeStruct((m, vw), BF16),
        scratch_shapes=[pltpu.VMEM((GLA_HEADS, GLA_DV, GLA_DK), F32)],
        compiler_params=_cparams(("parallel", "arbitrary"), 32),
        name="gla_attention",
    )(proj, proj, proj, proj, gc, norm_g)


def _gdn_gate_kernel(h_ref, gpre_ref, w_ref, alog_ref, dtb_ref, o_ref, *, n_heads):
    xn = _rms_rows(h_ref[...], gpre_ref[...]).astype(BF16)
    ba = jnp.dot(xn, w_ref[...], preferred_element_type=F32)
    beta = jax.nn.sigmoid(ba)
    g = -jnp.exp(alog_ref[...]) * _softplus(ba + dtb_ref[...])
    lane = lax.broadcasted_iota(jnp.int32, ba.shape, 1)
    o_ref[...] = jnp.where((lane & (LANES - 1)) < n_heads, beta, _chunk_cumsum(g))


def gdn_gate(h, gpre, w, alog, dtb, *, tm, n_heads):
    m, d = h.shape
    n = w.shape[1]
    return pl.pallas_call(
        functools.partial(_gdn_gate_kernel, n_heads=n_heads),
        grid=(m // tm,),
        in_specs=[pl.BlockSpec((tm, d), lambda i: (i, 0)),
                  pl.BlockSpec((1, d), lambda i: (0, 0)),
                  pl.BlockSpec((d, n), lambda i: (0, 0)),
                  pl.BlockSpec((1, n), lambda i: (0, 0)),
                  pl.BlockSpec((1, n), lambda i: (0, 0))],
        out_specs=pl.BlockSpec((tm, n), lambda i: (i, 0)),
        out_shape=jax.ShapeDtypeStruct((m, n), F32),
        compiler_params=_cparams(("parallel",), 32),
        name="gdn_gate",
    )(h, gpre, w, alog, dtb)


CONV_HALO = 8


def _conv_silu_to(o_ref, x_ref, prev_ref, w_ref, xs_ref, first, l2_normalize):
    tb, tc = x_ref.shape
    halo = CONV_HALO
    xs_ref[0:halo, :] = jnp.where(first, 0.0, prev_ref[...])
    xs_ref[halo:halo + tb, :] = x_ref[...]
    for r in range(tb // ROW_CHUNK):
        for hh in range(tc // GDN_DIM):
            cols = slice(hh * GDN_DIM, (hh + 1) * GDN_DIM)
            y = jnp.zeros((ROW_CHUNK, GDN_DIM), F32)
            for i in range(GDN_CONV_K):
                off = r * ROW_CHUNK + halo - (GDN_CONV_K - 1) + i
                y = y + w_ref[i:i + 1, cols] * xs_ref[off:off + ROW_CHUNK, cols]
            y = _silu(y)
            if l2_normalize:
                y = y * lax.rsqrt(jnp.sum(y * y, axis=-1, keepdims=True) + RMS_EPS)
            o_ref[r * ROW_CHUNK:(r + 1) * ROW_CHUNK, cols] = y.astype(o_ref.dtype)


def _gdn_kernel(qx_ref, kx_ref, vx_ref, qp_ref, kp_ref, vp_ref, qw_ref, kw_ref, vw_ref,
                z_ref, col_ref, grow_ref, glast_ref, ng_ref, o_ref,
                s_ref, u_ref, w_ref, qkd_ref, q16_ref, k16_ref, eg_ref, ed_ref, egl_ref,
                q_ref, k_ref, v_ref, qs_ref, ks_ref, vs_ref):
    first = pl.program_id(2) == 0

    @pl.when(first)
    def _():
        s_ref[...] = jnp.zeros_like(s_ref)

    _conv_silu_to(q_ref, qx_ref, qp_ref, qw_ref, qs_ref, first, True)
    _conv_silu_to(k_ref, kx_ref, kp_ref, kw_ref, ks_ref, first, True)
    _conv_silu_to(v_ref, vx_ref, vp_ref, vw_ref, vs_ref, first, False)

    groups, nc = u_ref.shape[0], u_ref.shape[1]
    n = groups * nc
    d = GDN_DIM
    c2 = 2 * CHUNK
    row = lax.broadcasted_iota(jnp.int32, (CHUNK, c2), 0)
    lane = lax.broadcasted_iota(jnp.int32, (CHUNK, c2), 1)
    first_head = lane < CHUNK
    key_pos = jnp.where(first_head, lane, lane - CHUNK)
    incl = key_pos <= row
    strict = key_pos < row
    diag = key_pos == row

    def chunked(ref, width):
        return jnp.concatenate([ref[:, g * width:(g + 1) * width].reshape(nc, CHUNK, width)
                                for g in range(groups)], axis=0)

    def block_diag(m16):
        zero = jnp.zeros_like(m16)
        return jnp.concatenate([jnp.where(first_head, m16, zero), jnp.where(first_head, zero, m16)], axis=1)

    q = chunked(q_ref, d).astype(F32) * (d ** -0.5)
    k = chunked(k_ref, d).astype(F32)
    v = chunked(v_ref, 2 * d).astype(F32)
    cols = col_ref[...].reshape(nc, CHUNK, LANES)
    nv = 2 * groups

    def token_column(lane0):
        return jnp.concatenate([cols[:, :, lane0 + 2 * g:lane0 + 2 * g + 1] for g in range(groups)], axis=0)

    beta_a, beta_b = token_column(0), token_column(1)
    g_a, g_b = token_column(nv), token_column(nv + 1)
    g_row = jnp.concatenate([grow_ref[:, g:g + 1, :] for g in range(groups)], axis=0)
    g_end = jnp.concatenate([glast_ref[:, g:g + 1, :] for g in range(groups)], axis=0)

    k16 = k.astype(BF16)
    q16 = q.astype(BF16)
    prod = _bmm_nt(jnp.concatenate([k16, q16], axis=1), jnp.concatenate([k16, k16], axis=1))
    kk2, qk2 = prod[:, :CHUNK], prod[:, CHUNK:]

    g_col2 = jnp.where(first_head, g_a, g_b)
    beta2 = jnp.where(first_head, beta_a, beta_b)
    decay2 = jnp.where(incl, jnp.exp(jnp.where(incl, g_col2 - g_row, 0.0)), 0.0)
    x = jnp.where(strict, -(beta2 * kk2 * decay2), 0.0)
    qkd_ref[...] = (qk2 * decay2).astype(BF16).reshape(qkd_ref.shape)

    r = jnp.zeros_like(x)
    p = jnp.where((row >> 3) == (key_pos >> 3), x, 0.0)
    for _ in range(3):
        p16 = p.astype(BF16)
        y = _bmm(jnp.concatenate([r.astype(BF16), p16], axis=1), block_diag(p16))
        r = r + p + y[:, :CHUNK]
        p = y[:, CHUNK:]
    tinv = r + jnp.where(diag, 1.0, 0.0)
    for shift in (3, 4, 5):
        couples = jnp.logical_and((row >> (shift + 1)) == (key_pos >> (shift + 1)),
                                  (row >> shift) != (key_pos >> shift))
        xc16 = jnp.where(couples, x, 0.0).astype(BF16)
        t16 = tinv.astype(BF16)
        y = _bmm(xc16, block_diag(t16))
        tinv = tinv + _bmm(t16, block_diag(y.astype(BF16)))
    r = jnp.where(diag, 0.0, tinv)

    eg_a, eg_b = jnp.exp(g_a), jnp.exp(g_b)
    vb_a, vb_b = v[:, :, :d] * beta_a, v[:, :, d:] * beta_b
    kg_a, kg_b = k * (beta_a * eg_a), k * (beta_b * eg_b)
    rhs2 = jnp.concatenate([jnp.concatenate([vb_a, kg_a], axis=2),
                            jnp.concatenate([vb_b, kg_b], axis=2)], axis=1).astype(BF16)
    corr = _bmm(block_diag(r.astype(BF16)), rhs2)
    u_ref[...] = jnp.concatenate([vb_a + corr[:, :CHUNK, :d], vb_b + corr[:, CHUNK:, :d]],
                                 axis=2).reshape(u_ref.shape)
    w_ref[...] = jnp.concatenate([kg_a + corr[:, :CHUNK, d:], kg_b + corr[:, CHUNK:, d:]],
                                 axis=1).astype(BF16).reshape(w_ref.shape)
    q16_ref[...] = q16.reshape(q16_ref.shape)
    k16_ref[...] = k16.reshape(k16_ref.shape)
    lanes_d = (n, CHUNK, d)
    eg_ref[...] = jnp.concatenate([jnp.broadcast_to(eg_a, lanes_d), jnp.broadcast_to(eg_b, lanes_d)],
                                  axis=2).reshape(eg_ref.shape)
    ed_ref[...] = jnp.concatenate([jnp.exp(g_end[:, :, :d] - g_a), jnp.exp(g_end[:, :, d:] - g_b)],
                                  axis=2).reshape(ed_ref.shape)
    egl_ref[...] = jnp.exp(g_end).reshape(egl_ref.shape)

    zeros_vd = jnp.zeros((groups, CHUNK, d), BF16)
    tn = (((1,), (1,)), ((0,), (0,)))
    for c in range(nc):
        rows = slice(c * CHUNK, (c + 1) * CHUNK)
        s = s_ref[...]
        s16 = s.astype(BF16)
        ws = _bmm(jnp.concatenate([w_ref[:, c], q16_ref[:, c]], axis=1), s16)
        vn = u_ref[:, c] - jnp.concatenate([ws[:, :CHUNK, :d], ws[:, CHUNK:c2, d:]], axis=2)
        vn16 = vn.astype(BF16)
        vn_bd = jnp.concatenate([jnp.concatenate([vn16[:, :, :d], zeros_vd], axis=2),
                                 jnp.concatenate([zeros_vd, vn16[:, :, d:]], axis=2)], axis=1)
        o = eg_ref[:, c] * ws[:, c2:] + _bmm(qkd_ref[:, c], vn_bd)
        vn_dec = (vn * ed_ref[:, c]).astype(BF16)
        s_ref[...] = s * egl_ref[:, c] + lax.dot_general(k16_ref[:, c], vn_dec, tn,
                                                         preferred_element_type=F32)
        for g in range(groups):
            for e in range(2):
                cs = slice((2 * g + e) * d, (2 * g + e + 1) * d)
                o_ref[rows, cs] = (_rms_rows(o[g, :, e * d:(e + 1) * d], ng_ref[...])
                                   * _silu(z_ref[rows, cs])).astype(o_ref.dtype)


def gdn_attention(proj, conv_w, colv, grow, glast, norm_g, *, batch, seq, tb, groups):
    m = proj.shape[0]
    nt = seq // tb
    nc = tb // CHUNK
    d = GDN_DIM
    kw, vw = groups * d, groups * 2 * d
    k0 = GDN_KH * d // kw
    v0 = 2 * GDN_KH * d // vw
    z0 = (2 * GDN_KH + GDN_VH) * d // vw
    per = lambda *shape: (groups, nc) + shape
    halo_blocks = tb // CONV_HALO
    prev_rows = lambda b, t: jnp.maximum((b * nt + t) * halo_blocks - 1, 0)
    return pl.pallas_call(
        _gdn_kernel,
        grid=(batch, GDN_KH // groups, nt),
        in_specs=[pl.BlockSpec((tb, kw), lambda b, h, t: (b * nt + t, h)),
                  pl.BlockSpec((tb, kw), lambda b, h, t: (b * nt + t, k0 + h)),
                  pl.BlockSpec((tb, vw), lambda b, h, t: (b * nt + t, v0 + h)),
                  pl.BlockSpec((CONV_HALO, kw), lambda b, h, t: (prev_rows(b, t), h)),
                  pl.BlockSpec((CONV_HALO, kw), lambda b, h, t: (prev_rows(b, t), k0 + h)),
                  pl.BlockSpec((CONV_HALO, vw), lambda b, h, t: (prev_rows(b, t), v0 + h)),
                  pl.BlockSpec((GDN_CONV_K, kw), lambda b, h, t: (0, h)),
                  pl.BlockSpec((GDN_CONV_K, kw), lambda b, h, t: (0, k0 + h)),
                  pl.BlockSpec((GDN_CONV_K, vw), lambda b, h, t: (0, v0 + h)),
                  pl.BlockSpec((tb, vw), lambda b, h, t: (b * nt + t, z0 + h)),
                  pl.BlockSpec((tb, LANES), lambda b, h, t: (b * nt + t, h)),
                  pl.BlockSpec((None, None, nc, 8, 2 * CHUNK), lambda b, h, t: (b, h, t, 0, 0)),
                  pl.BlockSpec((None, None, nc, 8, 2 * d), lambda b, h, t: (b, h, t, 0, 0)),
                  pl.BlockSpec((1, d), lambda b, h, t: (0, 0))],
        out_specs=pl.BlockSpec((tb, vw), lambda b, h, t: (b * nt + t, h)),
        out_shape=jax.ShapeDtypeStruct((m, GDN_VH * d), BF16),
        scratch_shapes=[pltpu.VMEM((groups, d, 2 * d), F32),
                        pltpu.VMEM(per(CHUNK, 2 * d), F32),
                        pltpu.VMEM(per(2 * CHUNK, d), BF16),
                        pltpu.VMEM(per(CHUNK, 2 * CHUNK), BF16),
                        pltpu.VMEM(per(CHUNK, d), BF16),
                        pltpu.VMEM(per(CHUNK, d), BF16),
                        pltpu.VMEM(per(CHUNK, 2 * d), F32),
                        pltpu.VMEM(per(CHUNK, 2 * d), F32),
                        pltpu.VMEM(per(1, 2 * d), F32),
                        pltpu.VMEM((tb, kw), F32),
                        pltpu.VMEM((tb, kw), F32),
                        pltpu.VMEM((tb, vw), F32),
                        pltpu.VMEM((tb + CONV_HALO, kw), F32),
                        pltpu.VMEM((tb + CONV_HALO, kw), F32),
                        pltpu.VMEM((tb + CONV_HALO, vw), F32)],
        compiler_params=_cparams(("parallel", "parallel", "arbitrary"), 32),
        name="gdn_attention",
    )(proj, proj, proj, proj, proj, proj, conv_w, conv_w, conv_w, proj, colv, grow, glast, norm_g)


def _pad_cols(w, n):
    return jnp.pad(w, ((0, 0), (0, n - w.shape[1])))


def _even_mixer(h, gpre, gpost, w_in, gate_w2, gate_b, gla_norm_g, w_out, *, batch, seq):
    n_sb = 3 * SB_HEADS * SB_DIM
    n_main = w_in.shape[1] - GLA_RANK
    sb_qkv = norm_matmul(h, gpre, w_in[:, :n_sb].astype(BF16), tm=1024, tn=1024, out_dtype=BF16)
    gla_proj = norm_matmul(h, gpre, w_in[:, n_sb:n_main].astype(BF16), tm=1024, tn=1024, out_dtype=F32)
    w1 = _pad_cols(w_in[:, n_main:], LANES).astype(BF16)
    w2 = jnp.pad(gate_w2, ((0, LANES - GLA_RANK), (0, 0))).astype(BF16)
    gc = gla_gate(h, gpre, w1, w2, gate_b[None, :], tm=256)
    o_sb = sb_attention(sb_qkv, batch=batch, seq=seq, tq=128)
    o_gla = gla_attention(gla_proj, gc, gla_norm_g[None, :], batch=batch, seq=seq, tb=256)
    return matmul_post((o_sb, o_gla), w_out.astype(BF16), gpost, h, tm=512)


def _odd_mixer(h, gpre, gpost, w_in, conv_w, a_log, dt_bias, norm_g, w_out, *, batch, seq):
    n_main = w_in.shape[1] - 2 * GDN_VH
    proj = norm_matmul(h, gpre, w_in[:, :n_main].astype(BF16), tm=1024, tn=1024, out_dtype=F32)
    groups = 4
    ng, nv = GDN_KH // groups, 2 * groups
    d_model = w_in.shape[0]
    slab = lambda first, second, lead: jnp.concatenate(
        [first.reshape(lead + (ng, nv)), second.reshape(lead + (ng, nv)),
         jnp.zeros(lead + (ng, LANES - 2 * nv), F32)], axis=-1).reshape(lead + (ng * LANES,))
    w_ba = slab(w_in[:, n_main:n_main + GDN_VH], w_in[:, n_main + GDN_VH:], (d_model,)).astype(BF16)
    zeros_vh = jnp.zeros((GDN_VH,), F32)
    gates = gdn_gate(h, gpre, w_ba, slab(zeros_vh, a_log, ())[None, :], slab(zeros_vh, dt_bias, ())[None, :],
                     tm=256, n_heads=nv)

    nc = seq // CHUNK
    gcum = gates.reshape(batch, nc, CHUNK, ng, LANES)[..., nv:2 * nv].reshape(batch, nc, CHUNK, ng, groups, 2)
    pad_sublanes = lambda a: jnp.pad(a, ((0, 0), (0, 0), (0, 0), (0, 8 - groups), (0, 0)))
    grow = pad_sublanes(gcum.transpose(0, 3, 1, 4, 5, 2).reshape(batch, ng, nc, groups, 2 * CHUNK))
    glast = pad_sublanes(jnp.repeat(gcum[:, :, CHUNK - 1].transpose(0, 2, 1, 3, 4), GDN_DIM, axis=-1))

    o = gdn_attention(proj, conv_w, gates, grow, glast, norm_g[None, :], batch=batch, seq=seq, tb=256,
                      groups=groups)
    return matmul_post((o,), w_out.astype(BF16), gpost, h, tm=512)


def kernel(x, mem, mem_norm_g, mem_w_kv, mix_pre_g, mix_post_g, ab_w_in, gla_gate_w2, gla_gate_b, gla_norm_g, ab_w_out, gdn_w_in, gdn_conv_w, gdn_a_log, gdn_dt_bias, gdn_norm_g, gdn_w_out, xattn_pre_g, xattn_post_g, xattn_w_q, xattn_w_o, ffn_pre_g, ffn_post_g, ffn_w_gate_up, ffn_w_down):
    batch, seq, d = x.shape
    depth = mix_pre_g.shape[0]
    mem_tokens = mem.shape[1]
    row = lambda g: g[None, :]

    kv = norm_matmul(mem.reshape(batch * mem_tokens, d), row(mem_norm_g), mem_w_kv.astype(BF16),
                     tm=batch * mem_tokens, tn=1024, out_dtype=BF16)

    h = x.reshape(batch * seq, d)
    for layer in range(depth):
        j = layer // 2
        gpre, gpost = row(mix_pre_g[layer]), row(mix_post_g[layer])
        if layer % 2 == 0:
            h = _even_mixer(h, gpre, gpost, ab_w_in[j], gla_gate_w2[j], gla_gate_b[j], gla_norm_g[j],
                            ab_w_out[j], batch=batch, seq=seq)
        else:
            h = _odd_mixer(h, gpre, gpost, gdn_w_in[j], gdn_conv_w[j], gdn_a_log[j], gdn_dt_bias[j],
                           gdn_norm_g[j], gdn_w_out[j], batch=batch, seq=seq)
        h = xattn_layer(h, row(xattn_pre_g[layer]), xattn_w_q[layer].astype(BF16), kv,
                        xattn_w_o[layer].astype(BF16), row(xattn_post_g[layer]), tm=512, seq=seq)
        h = ffn_layer(h, row(ffn_pre_g[layer]), ffn_w_gate_up[layer].astype(BF16),
                      ffn_w_down[layer].astype(BF16), row(ffn_post_g[layer]), tm=512, tf=512)
    return h.reshape(batch, seq, d)
```

```python
import functools

import jax
import jax.numpy as jnp
from jax import lax
from jax.experimental import pallas as pl
from jax.experimental.pallas import tpu as pltpu

F32 = jnp.float32
BF16 = jnp.bfloat16
RMS_EPS = 1e-6

SB_HEADS = 8
SB_DIM = 128
GLA_HEADS = 4
GLA_DK = 128
GLA_DV = 256
GLA_RANK = 16
GLA_NORMALIZER = 16.0
GDN_KH = 16
GDN_VH = 32
GDN_DIM = 128
GDN_CONV_K = 4
MEM_HEADS = 4
CHUNK = 64
LANES = 128
ROW_CHUNK = 64
SB_LOG_CUTOFF = -104.0


def _cparams(semantics, vmem_mib):
    return pltpu.CompilerParams(dimension_semantics=semantics, vmem_limit_bytes=vmem_mib << 20)


def _rms_rows(x, g):
    ms = jnp.mean(x * x, axis=-1, keepdims=True)
    return x * lax.rsqrt(ms + RMS_EPS) * g


def _softplus(z):
    return jnp.maximum(z, 0.0) + jnp.log1p(jnp.exp(-jnp.abs(z)))


def _silu(x):
    half = 0.5 * x
    return half + half * jnp.tanh(half)


def _split3(x):
    hi = x.astype(BF16)
    r1 = x - hi.astype(F32)
    mid = r1.astype(BF16)
    lo = (r1 - mid.astype(F32)).astype(BF16)
    return hi, mid, lo


def _for_rows(n_rows, body):
    def step(r, c):
        body(pl.ds(pl.multiple_of(r * ROW_CHUNK, ROW_CHUNK), ROW_CHUNK))
        return c
    lax.fori_loop(0, n_rows // ROW_CHUNK, step, 0, unroll=2)


def _prenorm_to(xn_ref, h_ref, g_ref):
    def body(rows):
        xn_ref[rows, :] = _rms_rows(h_ref[rows, :], g_ref[...]).astype(xn_ref.dtype)
    _for_rows(h_ref.shape[0], body)


def _postnorm_residual_to(o_ref, h_ref, acc_ref, g_ref):
    def body(rows):
        o_ref[rows, :] = h_ref[rows, :] + _rms_rows(acc_ref[rows, :], g_ref[...])
    _for_rows(h_ref.shape[0], body)


GATE_ROWS = 256


def _norm_matmul_kernel(h_ref, g_ref, w_ref, *rest, gate_rows):
    xn_ref = rest[-1]
    if gate_rows is None:
        (o_ref,) = rest[:-1]
    else:
        gate_in, o_ref, gate_o_ref = rest[:-3], rest[-3], rest[-2]

    @pl.when(pl.program_id(1) == 0)
    def _():
        _prenorm_to(xn_ref, h_ref, g_ref)
        if gate_rows is not None:
            for r in range(xn_ref.shape[0] // GATE_ROWS):
                rows = slice(r * GATE_ROWS, (r + 1) * GATE_ROWS)
                gate_o_ref[rows, :] = gate_rows(xn_ref[rows, :], *gate_in)

    o_ref[...] = jnp.dot(xn_ref[...], w_ref[...], preferred_element_type=F32).astype(o_ref.dtype)


def norm_matmul(h, g, w, *, tm, tn, out_dtype, gate=None):
    m, d = h.shape
    n = w.shape[1]
    gate_rows, gate_ops, gate_cols = gate if gate is not None else (None, (), 0)
    out_specs = [pl.BlockSpec((tm, tn), lambda i, j: (i, j))]
    out_shape = [jax.ShapeDtypeStruct((m, n), out_dtype)]
    if gate is not None:
        out_specs.append(pl.BlockSpec((tm, gate_cols), lambda i, j: (i, 0)))
        out_shape.append(jax.ShapeDtypeStruct((m, gate_cols), F32))
    outs = pl.pallas_call(
        functools.partial(_norm_matmul_kernel, gate_rows=gate_rows),
        grid=(m // tm, n // tn),
        in_specs=[pl.BlockSpec((tm, d), lambda i, j: (i, 0)),
                  pl.BlockSpec((1, d), lambda i, j: (0, 0)),
                  pl.BlockSpec((d, tn), lambda i, j: (0, j))]
                 + [pl.BlockSpec(a.shape, lambda i, j: (0, 0)) for a in gate_ops],
        out_specs=out_specs,
        out_shape=out_shape,
        scratch_shapes=[pltpu.VMEM((tm, d), BF16)],
        compiler_params=_cparams(("parallel", "arbitrary"), 48),
        name="norm_matmul",
    )(h, g, w, *gate_ops)
    return outs if gate is not None else outs[0]


def _matmul_post_kernel(*refs):
    a_refs, (w_ref, g_ref, h_ref, o_ref, acc_ref) = refs[:-5], refs[-5:]
    k0 = 0
    for idx, a_ref in enumerate(a_refs):
        kw = a_ref.shape[1]
        part = jnp.dot(a_ref[...], w_ref[k0:k0 + kw, :], preferred_element_type=F32)
        if idx == 0:
            acc_ref[...] = part
        else:
            acc_ref[...] += part
        k0 += kw
    _postnorm_residual_to(o_ref, h_ref, acc_ref, g_ref)


def matmul_post(a_slabs, w, g, h, *, tm):
    m = h.shape[0]
    k, d = w.shape
    return pl.pallas_call(
        _matmul_post_kernel,
        grid=(m // tm,),
        in_specs=[pl.BlockSpec((tm, a.shape[1]), lambda i: (i, 0)) for a in a_slabs] + [
                  pl.BlockSpec((k, d), lambda i: (0, 0), pipeline_mode=pl.Buffered(1)),
                  pl.BlockSpec((1, d), lambda i: (0, 0)),
                  pl.BlockSpec((tm, d), lambda i: (i, 0))],
        out_specs=pl.BlockSpec((tm, d), lambda i: (i, 0)),
        out_shape=jax.ShapeDtypeStruct((m, d), F32),
        scratch_shapes=[pltpu.VMEM((tm, d), F32)],
        compiler_params=_cparams(("parallel",), ((2 * tm * k * 2 + k * d * 2 + 5 * tm * d * 4) >> 20) + 8),
        name="matmul_post",
    )(*a_slabs, w, g, h)


def _ffn_kernel(h_ref, gpre_ref, wg_ref, wu_ref, wd_ref, gpost_ref, o_ref, xn_ref, acc_ref):
    j = pl.program_id(1)

    @pl.when(j == 0)
    def _():
        _prenorm_to(xn_ref, h_ref, gpre_ref)
        acc_ref[...] = jnp.zeros_like(acc_ref)

    xn = xn_ref[...]
    gate = jnp.dot(xn, wg_ref[...], preferred_element_type=F32)
    up = jnp.dot(xn, wu_ref[...], preferred_element_type=F32)
    act = (_silu(gate) * up).astype(BF16)
    acc_ref[...] += jnp.dot(act, wd_ref[...], preferred_element_type=F32)

    @pl.when(j == pl.num_programs(1) - 1)
    def _():
        _postnorm_residual_to(o_ref, h_ref, acc_ref, gpost_ref)


def ffn_layer(h, gpre, w_gate_up, w_down, gpost, *, tm, tf):
    m, d = h.shape
    f = w_down.shape[0]
    nf = f // tf
    return pl.pallas_call(
        _ffn_kernel,
        grid=(m // tm, nf),
        in_specs=[pl.BlockSpec((tm, d), lambda i, j: (i, 0)),
                  pl.BlockSpec((1, d), lambda i, j: (0, 0)),
                  pl.BlockSpec((d, tf), lambda i, j: (0, j)),
                  pl.BlockSpec((d, tf), lambda i, j: (0, j + nf)),
                  pl.BlockSpec((tf, d), lambda i, j: (j, 0)),
                  pl.BlockSpec((1, d), lambda i, j: (0, 0))],
        out_specs=pl.BlockSpec((tm, d), lambda i, j: (i, 0)),
        out_shape=jax.ShapeDtypeStruct((m, d), F32),
        scratch_shapes=[pltpu.VMEM((tm, d), BF16), pltpu.VMEM((tm, d), F32)],
        compiler_params=_cparams(("parallel", "arbitrary"), 48),
        name="ffn_layer",
    )(h, gpre, w_gate_up, w_gate_up, w_down, gpost)


def _xattn_kernel(h_ref, gpre_ref, wq_ref, k_ref, v_ref, wo_ref, gpost_ref, o_ref, xn_ref, att_ref, acc_ref):
    _prenorm_to(xn_ref, h_ref, gpre_ref)
    d = h_ref.shape[1]
    hd = d // MEM_HEADS
    xn = xn_ref[...]
    for i in range(MEM_HEADS):
        cols = slice(i * hd, (i + 1) * hd)
        q = jnp.dot(xn, wq_ref[:, cols], preferred_element_type=F32)
        s = lax.dot_general(q.astype(BF16), k_ref[:, cols], (((1,), (1,)), ((), ())),
                            preferred_element_type=F32) * (hd ** -0.5)
        e = jnp.exp(s - jnp.max(s, axis=-1, keepdims=True))
        p = e / jnp.sum(e, axis=-1, keepdims=True)
        att_ref[:, cols] = jnp.dot(p.astype(BF16), v_ref[:, cols], preferred_element_type=F32).astype(BF16)
    acc_ref[...] = jnp.dot(att_ref[...], wo_ref[...], preferred_element_type=F32)
    _postnorm_residual_to(o_ref, h_ref, acc_ref, gpost_ref)


def xattn_layer(h, gpre, wq, kv, wo, gpost, *, tm, seq):
    m, d = h.shape
    mt = kv.shape[0] // (m // seq)
    per_b = seq // tm
    once = pl.Buffered(1)
    vmem = 4 * tm * d * 4 + 2 * d * d * 2 + 4 * mt * d * 2 + 2 * tm * d * 2 + tm * d * 4
    return pl.pallas_call(
        _xattn_kernel,
        grid=(m // tm,),
        in_specs=[pl.BlockSpec((tm, d), lambda i: (i, 0)),
                  pl.BlockSpec((1, d), lambda i: (0, 0)),
                  pl.BlockSpec((d, d), lambda i: (0, 0), pipeline_mode=once),
                  pl.BlockSpec((mt, d), lambda i: (i // per_b, 0)),
                  pl.BlockSpec((mt, d), lambda i: (i // per_b, 1)),
                  pl.BlockSpec((d, d), lambda i: (0, 0), pipeline_mode=once),
                  pl.BlockSpec((1, d), lambda i: (0, 0))],
        out_specs=pl.BlockSpec((tm, d), lambda i: (i, 0)),
        out_shape=jax.ShapeDtypeStruct((m, d), F32),
        scratch_shapes=[pltpu.VMEM((tm, d), BF16), pltpu.VMEM((tm, d), BF16), pltpu.VMEM((tm, d), F32)],
        compiler_params=_cparams(("parallel",), (vmem >> 20) + 10),
        name="xattn_layer",
    )(h, gpre, wq, kv, kv, wo, gpost)


def _split_heads(x, n, width):
    return jnp.stack([x[:, i * width:(i + 1) * width] for i in range(n)], axis=0)


def _bmm(a, b):
    return lax.dot_general(a, b, (((2,), (1,)), ((0,), (0,))), preferred_element_type=F32)


def _bmm_nt(a, b):
    return lax.dot_general(a, b, (((2,), (2,)), ((0,), (0,))), preferred_element_type=F32)


def _sb_kernel(q_ref, k_ref, v_ref, o_ref, carry_ref, acc_ref):
    tq = q_ref.shape[0]
    nh = SB_HEADS
    qi = pl.program_id(1)
    scale = SB_DIM ** -0.5
    q = _split_heads(q_ref[...], nh, SB_DIM)
    row = lax.broadcasted_iota(jnp.int32, (tq, tq), 0)
    col = lax.broadcasted_iota(jnp.int32, (tq, tq), 1)
    after = jnp.where(row > col, 1.0, 0.0).astype(BF16)

    def block(j, diagonal):
        ks = pl.ds(pl.multiple_of(j * tq, tq), tq)
        kb = _split_heads(k_ref[ks, :], nh, SB_DIM)
        vb = _split_heads(v_ref[ks, :], nh, SB_DIM)
        z = _bmm_nt(q, kb) * scale
        sp = _softplus(z)
        log_keep = -sp
        if diagonal:
            valid = col < row
            log_keep = jnp.where(valid, log_keep, 0.0)
        hi, mid, lo = _split3(log_keep.reshape(nh * tq, tq))
        later = (jnp.dot(hi, after, preferred_element_type=F32)
                 + jnp.dot(mid, after, preferred_element_type=F32)
                 + jnp.dot(lo, after, preferred_element_type=F32)).reshape(nh, tq, tq)
        w = jnp.exp(z - sp + later + carry_ref[...])
        if diagonal:
            w = jnp.where(valid, w, 0.0)
        acc_ref[...] += _bmm(w.astype(BF16), vb)
        carry_ref[...] += jnp.sum(log_keep, axis=-1, keepdims=True)
        return jnp.max(carry_ref[...])

    carry_ref[...] = jnp.zeros_like(carry_ref)
    acc_ref[...] = jnp.zeros_like(acc_ref)
    top = block(qi, True)

    def cond(state):
        j, top = state
        return jnp.logical_and(j >= 0, top > SB_LOG_CUTOFF)

    def body(state):
        j, _ = state
        return j - 1, block(j, False)

    lax.while_loop(cond, body, (qi - 1, top))
    for i in range(nh):
        o_ref[:, i * SB_DIM:(i + 1) * SB_DIM] = acc_ref[i].astype(o_ref.dtype)


def sb_attention(qkv, *, batch, seq, tq):
    m = qkv.shape[0]
    nq = seq // tq
    width = SB_HEADS * SB_DIM
    return pl.pallas_call(
        _sb_kernel,
        grid=(batch, nq),
        in_specs=[pl.BlockSpec((tq, width), lambda b, i: (b * nq + i, 0)),
                  pl.BlockSpec((seq, width), lambda b, i: (b, 1)),
                  pl.BlockSpec((seq, width), lambda b, i: (b, 2))],
        out_specs=pl.BlockSpec((tq, width), lambda b, i: (b * nq + i, 0)),
        out_shape=jax.ShapeDtypeStruct((m, width), BF16),
        scratch_shapes=[pltpu.VMEM((SB_HEADS, tq, 1), F32), pltpu.VMEM((SB_HEADS, tq, SB_DIM), F32)],
        compiler_params=_cparams(("parallel", "arbitrary"), 32),
        name="sb_attention",
    )(qkv, qkv, qkv)


def _chunk_cumsum_matrix(n):
    row = lax.broadcasted_iota(jnp.int32, (n, n), 0)
    col = lax.broadcasted_iota(jnp.int32, (n, n), 1)
    same = (row // CHUNK) == (col // CHUNK)
    return jnp.where(jnp.logical_and(same, col <= row), 1.0, 0.0).astype(BF16)


def _chunk_cumsum(x):
    tri = _chunk_cumsum_matrix(x.shape[0])
    hi, mid, lo = _split3(x)
    return (jnp.dot(tri, hi, preferred_element_type=F32)
            + jnp.dot(tri, mid, preferred_element_type=F32)
            + jnp.dot(tri, lo, preferred_element_type=F32))


def _gla_gate_rows(xn, w1_ref, w2_ref, b_ref):
    low = jnp.dot(xn, w1_ref[...], preferred_element_type=F32)
    pre = jnp.dot(low.astype(BF16), w2_ref[...], preferred_element_type=F32) + b_ref[...]
    log_a = -_softplus(-pre) * (1.0 / GLA_NORMALIZER)
    return _chunk_cumsum(log_a)


def _gla_kernel(q_ref, k_ref, v_ref, r_ref, gc_ref, ng_ref, o_ref, st_ref):
    @pl.when(pl.program_id(1) == 0)
    def _():
        st_ref[...] = jnp.zeros_like(st_ref)

    tb = q_ref.shape[0]
    nh = GLA_HEADS
    row = lax.broadcasted_iota(jnp.int32, (CHUNK, CHUNK), 0)
    col = lax.broadcasted_iota(jnp.int32, (CHUNK, CHUNK), 1)
    causal = col <= row
    tn = (((1,), (1,)), ((0,), (0,)))
    for c in range(tb // CHUNK):
        rows = slice(c * CHUNK, (c + 1) * CHUNK)
        g = _split_heads(gc_ref[rows, :], nh, GLA_DK)
        g_last = g[:, CHUNK - 1:CHUNK, :]
        k = _split_heads(k_ref[rows, :], nh, GLA_DK)
        v = _split_heads(v_ref[rows, :], nh, GLA_DV).astype(BF16)
        q_dec = (_split_heads(q_ref[rows, :], nh, GLA_DK) * (GLA_DK ** -0.5) * jnp.exp(g)).astype(BF16)
        k_inv = (k * jnp.exp(-g)).astype(BF16)
        k_dec = (k * jnp.exp(g_last - g)).astype(BF16)
        scores = jnp.where(causal, _bmm_nt(q_dec, k_inv), 0.0).astype(BF16)
        st = st_ref[...]
        o = _bmm(scores, v) + _bmm_nt(q_dec, st.astype(BF16))
        st_ref[...] = st * jnp.exp(g_last) + lax.dot_general(v, k_dec, tn, preferred_element_type=F32)
        y = _rms_rows(o, ng_ref[...]) * _silu(_split_heads(r_ref[rows, :], nh, GLA_DV))
        for i in range(nh):
            o_ref[rows, i * GLA_DV:(i + 1) * GLA_DV] = y[i].astype(o_ref.dtype)


def gla_attention(proj, gc, norm_g, *, batch, seq, tb):
    m = proj.shape[0]
    nt = seq // tb
    kw, vw = GLA_HEADS * GLA_DK, GLA_HEADS * GLA_DV
    return pl.pallas_call(
        _gla_kernel,
        grid=(batch, nt),
        in_specs=[pl.BlockSpec((tb, kw), lambda b, t: (b * nt + t, 0)),
                  pl.BlockSpec((tb, kw), lambda b, t: (b * nt + t, 1)),
                  pl.BlockSpec((tb, vw), lambda b, t: (b * nt + t, 2 * kw // vw)),
                  pl.BlockSpec((tb, vw), lambda b, t: (b * nt + t, 2 * kw // vw + 1)),
                  pl.BlockSpec((tb, kw), lambda b, t: (b * nt + t, 0)),
                  pl.BlockSpec((1, GLA_DV), lambda b, t: (0, 0))],
        out_specs=pl.BlockSpec((tb, vw), lambda b, t: (b * nt + t, 0)),
        out_shape=jax.ShapeDtypeStruct((m, vw), BF16),
        scratch_shapes=[pltpu.VMEM((GLA_HEADS, GLA_DV, GLA_DK), F32)],
        compiler_params=_cparams(("parallel", "arbitrary"), 32),
        name="gla_attention",
    )(proj, proj, proj, proj, gc, norm_g)


def _gdn_gate_rows(xn, w_ref, alog_ref, dtb_ref, *, n_heads):
    ba = jnp.dot(xn, w_ref[...], preferred_element_type=F32)
    beta = jax.nn.sigmoid(ba)
    g = -jnp.exp(alog_ref[...]) * _softplus(ba + dtb_ref[...])
    lane = lax.broadcasted_iota(jnp.int32, ba.shape, 1)
    return jnp.where((lane & (LANES - 1)) < n_heads, beta, _chunk_cumsum(g))


CONV_HALO = 8


def _conv_silu_to(o_ref, x_ref, prev_ref, w_ref, xs_ref, first, l2_normalize):
    tb, tc = x_ref.shape
    halo = CONV_HALO
    xs_ref[0:halo, :] = jnp.where(first, 0.0, prev_ref[...])
    xs_ref[halo:halo + tb, :] = x_ref[...]
    for r in range(tb // ROW_CHUNK):
        for hh in range(tc // GDN_DIM):
            cols = slice(hh * GDN_DIM, (hh + 1) * GDN_DIM)
            y = jnp.zeros((ROW_CHUNK, GDN_DIM), F32)
            for i in range(GDN_CONV_K):
                off = r * ROW_CHUNK + halo - (GDN_CONV_K - 1) + i
                y = y + w_ref[i:i + 1, cols] * xs_ref[off:off + ROW_CHUNK, cols]
            y = _silu(y)
            if l2_normalize:
                y = y * lax.rsqrt(jnp.sum(y * y, axis=-1, keepdims=True) + RMS_EPS)
            o_ref[r * ROW_CHUNK:(r + 1) * ROW_CHUNK, cols] = y.astype(o_ref.dtype)


def _gdn_kernel(qx_ref, kx_ref, vx_ref, qp_ref, kp_ref, vp_ref, qw_ref, kw_ref, vw_ref,
                z_ref, col_ref, grow_ref, glast_ref, ng_ref, o_ref,
                s_ref, u_ref, w_ref, qkd_ref, q16_ref, k16_ref, eg_ref, ed_ref, egl_ref,
                q_ref, k_ref, v_ref, qs_ref, ks_ref, vs_ref):
    first = pl.program_id(2) == 0

    @pl.when(first)
    def _():
        s_ref[...] = jnp.zeros_like(s_ref)

    _conv_silu_to(q_ref, qx_ref, qp_ref, qw_ref, qs_ref, first, True)
    _conv_silu_to(k_ref, kx_ref, kp_ref, kw_ref, ks_ref, first, True)
    _conv_silu_to(v_ref, vx_ref, vp_ref, vw_ref, vs_ref, first, False)

    groups, nc = u_ref.shape[0], u_ref.shape[1]
    n = groups * nc
    d = GDN_DIM
    c2 = 2 * CHUNK
    row = lax.broadcasted_iota(jnp.int32, (CHUNK, c2), 0)
    lane = lax.broadcasted_iota(jnp.int32, (CHUNK, c2), 1)
    first_head = lane < CHUNK
    key_pos = jnp.where(first_head, lane, lane - CHUNK)
    incl = key_pos <= row
    strict = key_pos < row
    diag = key_pos == row

    def chunked(ref, width):
        return jnp.concatenate([ref[:, g * width:(g + 1) * width].reshape(nc, CHUNK, width)
                                for g in range(groups)], axis=0)

    def block_diag(m16):
        zero = jnp.zeros_like(m16)
        return jnp.concatenate([jnp.where(first_head, m16, zero), jnp.where(first_head, zero, m16)], axis=1)

    q = chunked(q_ref, d).astype(F32) * (d ** -0.5)
    k = chunked(k_ref, d).astype(F32)
    v = chunked(v_ref, 2 * d).astype(F32)
    cols = col_ref[...].reshape(nc, CHUNK, LANES)
    nv = 2 * groups

    def token_column(lane0):
        return jnp.concatenate([cols[:, :, lane0 + 2 * g:lane0 + 2 * g + 1] for g in range(groups)], axis=0)

    beta_a, beta_b = token_column(0), token_column(1)
    g_a, g_b = token_column(nv), token_column(nv + 1)
    g_row = jnp.concatenate([grow_ref[:, g:g + 1, :] for g in range(groups)], axis=0)
    g_end = jnp.concatenate([glast_ref[:, g:g + 1, :] for g in range(groups)], axis=0)

    k16 = k.astype(BF16)
    q16 = q.astype(BF16)
    prod = _bmm_nt(jnp.concatenate([k16, q16], axis=1), jnp.concatenate([k16, k16], axis=1))
    kk2, qk2 = prod[:, :CHUNK], prod[:, CHUNK:]

    g_col2 = jnp.where(first_head, g_a, g_b)
    beta2 = jnp.where(first_head, beta_a, beta_b)
    decay2 = jnp.where(incl, jnp.exp(jnp.where(incl, g_col2 - g_row, 0.0)), 0.0)
    x = jnp.where(strict, -(beta2 * kk2 * decay2), 0.0)
    qkd_ref[...] = (qk2 * decay2).astype(BF16).reshape(qkd_ref.shape)

    r = jnp.zeros_like(x)
    p = jnp.where((row >> 3) == (key_pos >> 3), x, 0.0)
    for _ in range(3):
        p16 = p.astype(BF16)
        y = _bmm(jnp.concatenate([r.astype(BF16), p16], axis=1), block_diag(p16))
        r = r + p + y[:, :CHUNK]
        p = y[:, CHUNK:]
    tinv = r + jnp.where(diag, 1.0, 0.0)
    for shift in (3, 4, 5):
        couples = jnp.logical_and((row >> (shift + 1)) == (key_pos >> (shift + 1)),
                                  (row >> shift) != (key_pos >> shift))
        xc16 = jnp.where(couples, x, 0.0).astype(BF16)
        t16 = tinv.astype(BF16)
        y = _bmm(xc16, block_diag(t16))
        tinv = tinv + _bmm(t16, block_diag(y.astype(BF16)))
    r = jnp.where(diag, 0.0, tinv)

    eg_a, eg_b = jnp.exp(g_a), jnp.exp(g_b)
    vb_a, vb_b = v[:, :, :d] * beta_a, v[:, :, d:] * beta_b
    kg_a, kg_b = k * (beta_a * eg_a), k * (beta_b * eg_b)
    rhs2 = jnp.concatenate([jnp.concatenate([vb_a, kg_a], axis=2),
                            jnp.concatenate([vb_b, kg_b], axis=2)], axis=1).astype(BF16)
    corr = _bmm(block_diag(r.astype(BF16)), rhs2)
    u_ref[...] = jnp.concatenate([vb_a + corr[:, :CHUNK, :d], vb_b + corr[:, CHUNK:, :d]],
                                 axis=2).reshape(u_ref.shape)
    w_ref[...] = jnp.concatenate([kg_a + corr[:, :CHUNK, d:], kg_b + corr[:, CHUNK:, d:]],
                                 axis=1).astype(BF16).reshape(w_ref.shape)
    q16_ref[...] = q16.reshape(q16_ref.shape)
    k16_ref[...] = k16.reshape(k16_ref.shape)
    lanes_d = (n, CHUNK, d)
    eg_ref[...] = jnp.concatenate([jnp.broadcast_to(eg_a, lanes_d), jnp.broadcast_to(eg_b, lanes_d)],
                                  axis=2).reshape(eg_ref.shape)
    ed_ref[...] = jnp.concatenate([jnp.exp(g_end[:, :, :d] - g_a), jnp.exp(g_end[:, :, d:] - g_b)],
                                  axis=2).reshape(ed_ref.shape)
    egl_ref[...] = jnp.exp(g_end).reshape(egl_ref.shape)

    zeros_vd = jnp.zeros((groups, CHUNK, d), BF16)
    tn = (((1,), (1,)), ((0,), (0,)))
    for c in range(nc):
        rows = slice(c * CHUNK, (c + 1) * CHUNK)
        s = s_ref[...]
        s16 = s.astype(BF16)
        ws = _bmm(jnp.concatenate([w_ref[:, c], q16_ref[:, c]], axis=1), s16)
        vn = u_ref[:, c] - jnp.concatenate([ws[:, :CHUNK, :d], ws[:, CHUNK:c2, d:]], axis=2)
        vn16 = vn.astype(BF16)
        vn_bd = jnp.concatenate([jnp.concatenate([vn16[:, :, :d], zeros_vd], axis=2),
                                 jnp.concatenate([zeros_vd, vn16[:, :, d:]], axis=2)], axis=1)
        o = eg_ref[:, c] * ws[:, c2:] + _bmm(qkd_ref[:, c], vn_bd)
        vn_dec = (vn * ed_ref[:, c]).astype(BF16)
        s_ref[...] = s * egl_ref[:, c] + lax.dot_general(k16_ref[:, c], vn_dec, tn,
                                                         preferred_element_type=F32)
        for g in range(groups):
            for e in range(2):
                cs = slice((2 * g + e) * d, (2 * g + e + 1) * d)
                o_ref[rows, cs] = (_rms_rows(o[g, :, e * d:(e + 1) * d], ng_ref[...])
                                   * _silu(z_ref[rows, cs])).astype(o_ref.dtype)


def gdn_attention(proj, conv_w, colv, grow, glast, norm_g, *, batch, seq, tb, groups):
    m = proj.shape[0]
    nt = seq // tb
    nc = tb // CHUNK
    d = GDN_DIM
    kw, vw = groups * d, groups * 2 * d
    k0 = GDN_KH * d // kw
    v0 = 2 * GDN_KH * d // vw
    z0 = (2 * GDN_KH + GDN_VH) * d // vw
    per = lambda *shape: (groups, nc) + shape
    halo_blocks = tb // CONV_HALO
    prev_rows = lambda b, t: jnp.maximum((b * nt + t) * halo_blocks - 1, 0)
    return pl.pallas_call(
        _gdn_kernel,
        grid=(batch, GDN_KH // groups, nt),
        in_specs=[pl.BlockSpec((tb, kw), lambda b, h, t: (b * nt + t, h)),
                  pl.BlockSpec((tb, kw), lambda b, h, t: (b * nt + t, k0 + h)),
                  pl.BlockSpec((tb, vw), lambda b, h, t: (b * nt + t, v0 + h)),
                  pl.BlockSpec((CONV_HALO, kw), lambda b, h, t: (prev_rows(b, t), h)),
                  pl.BlockSpec((CONV_HALO, kw), lambda b, h, t: (prev_rows(b, t), k0 + h)),
                  pl.BlockSpec((CONV_HALO, vw), lambda b, h, t: (prev_rows(b, t), v0 + h)),
                  pl.BlockSpec((GDN_CONV_K, kw), lambda b, h, t: (0, h)),
                  pl.BlockSpec((GDN_CONV_K, kw), lambda b, h, t: (0, k0 + h)),
                  pl.BlockSpec((GDN_CONV_K, vw), lambda b, h, t: (0, v0 + h)),
                  pl.BlockSpec((tb, vw), lambda b, h, t: (b * nt + t, z0 + h)),
                  pl.BlockSpec((tb, LANES), lambda b, h, t: (b * nt + t, h)),
                  pl.BlockSpec((None, None, nc, 8, 2 * CHUNK), lambda b, h, t: (b, h, t, 0, 0)),
                  pl.BlockSpec((None, None, nc, 8, 2 * d), lambda b, h, t: (b, h, t, 0, 0)),
                  pl.BlockSpec((1, d), lambda b, h, t: (0, 0))],
        out_specs=pl.BlockSpec((tb, vw), lambda b, h, t: (b * nt + t, h)),
        out_shape=jax.ShapeDtypeStruct((m, GDN_VH * d), BF16),
        scratch_shapes=[pltpu.VMEM((groups, d, 2 * d), F32),
                        pltpu.VMEM(per(CHUNK, 2 * d), F32),
                        pltpu.VMEM(per(2 * CHUNK, d), BF16),
                        pltpu.VMEM(per(CHUNK, 2 * CHUNK), BF16),
                        pltpu.VMEM(per(CHUNK, d), BF16),
                        pltpu.VMEM(per(CHUNK, d), BF16),
                        pltpu.VMEM(per(CHUNK, 2 * d), F32),
                        pltpu.VMEM(per(CHUNK, 2 * d), F32),
                        pltpu.VMEM(per(1, 2 * d), F32),
                        pltpu.VMEM((tb, kw), F32),
                        pltpu.VMEM((tb, kw), F32),
                        pltpu.VMEM((tb, vw), F32),
                        pltpu.VMEM((tb + CONV_HALO, kw), F32),
                        pltpu.VMEM((tb + CONV_HALO, kw), F32),
                        pltpu.VMEM((tb + CONV_HALO, vw), F32)],
        compiler_params=_cparams(("parallel", "parallel", "arbitrary"), 32),
        name="gdn_attention",
    )(proj, proj, proj, proj, proj, proj, conv_w, conv_w, conv_w, proj, colv, grow, glast, norm_g)


def _pad_cols(w, n):
    return jnp.pad(w, ((0, 0), (0, n - w.shape[1])))


def _even_mixer(h, gpre, gpost, w_in, gate_w2, gate_b, gla_norm_g, w_out, *, batch, seq):
    n_sb = 3 * SB_HEADS * SB_DIM
    n_main = w_in.shape[1] - GLA_RANK
    sb_qkv = norm_matmul(h, gpre, w_in[:, :n_sb].astype(BF16), tm=1024, tn=1024, out_dtype=BF16)
    w1 = _pad_cols(w_in[:, n_main:], LANES).astype(BF16)
    w2 = jnp.pad(gate_w2, ((0, LANES - GLA_RANK), (0, 0))).astype(BF16)
    gla_proj, gc = norm_matmul(h, gpre, w_in[:, n_sb:n_main].astype(BF16), tm=1024, tn=1024, out_dtype=F32,
                               gate=(_gla_gate_rows, (w1, w2, gate_b[None, :]), gate_w2.shape[1]))
    o_sb = sb_attention(sb_qkv, batch=batch, seq=seq, tq=128)
    o_gla = gla_attention(gla_proj, gc, gla_norm_g[None, :], batch=batch, seq=seq, tb=256)
    return matmul_post((o_sb, o_gla), w_out.astype(BF16), gpost, h, tm=512)


def _odd_mixer(h, gpre, gpost, w_in, conv_w, a_log, dt_bias, norm_g, w_out, *, batch, seq):
    n_main = w_in.shape[1] - 2 * GDN_VH
    groups = 4
    ng, nv = GDN_KH // groups, 2 * groups
    d_model = w_in.shape[0]
    slab = lambda first, second, lead: jnp.concatenate(
        [first.reshape(lead + (ng, nv)), second.reshape(lead + (ng, nv)),
         jnp.zeros(lead + (ng, LANES - 2 * nv), F32)], axis=-1).reshape(lead + (ng * LANES,))
    w_ba = slab(w_in[:, n_main:n_main + GDN_VH], w_in[:, n_main + GDN_VH:], (d_model,)).astype(BF16)
    zeros_vh = jnp.zeros((GDN_VH,), F32)
    gate_ops = (w_ba, slab(zeros_vh, a_log, ())[None, :], slab(zeros_vh, dt_bias, ())[None, :])
    proj, gates = norm_matmul(h, gpre, w_in[:, :n_main].astype(BF16), tm=1024, tn=1024, out_dtype=F32,
                              gate=(functools.partial(_gdn_gate_rows, n_heads=nv), gate_ops, ng * LANES))

    nc = seq // CHUNK
    gcum = gates.reshape(batch, nc, CHUNK, ng, LANES)[..., nv:2 * nv].reshape(batch, nc, CHUNK, ng, groups, 2)
    pad_sublanes = lambda a: jnp.pad(a, ((0, 0), (0, 0), (0, 0), (0, 8 - groups), (0, 0)))
    grow = pad_sublanes(gcum.transpose(0, 3, 1, 4, 5, 2).reshape(batch, ng, nc, groups, 2 * CHUNK))
    glast = pad_sublanes(jnp.repeat(gcum[:, :, CHUNK - 1].transpose(0, 2, 1, 3, 4), GDN_DIM, axis=-1))

    o = gdn_attention(proj, conv_w, gates, grow, glast, norm_g[None, :], batch=batch, seq=seq, tb=256,
                      groups=groups)
    return matmul_post((o,), w_out.astype(BF16), gpost, h, tm=512)


def kernel(x, mem, mem_norm_g, mem_w_kv, mix_pre_g, mix_post_g, ab_w_in, gla_gate_w2, gla_gate_b, gla_norm_g, ab_w_out, gdn_w_in, gdn_conv_w, gdn_a_log, gdn_dt_bias, gdn_norm_g, gdn_w_out, xattn_pre_g, xattn_post_g, xattn_w_q, xattn_w_o, ffn_pre_g, ffn_post_g, ffn_w_gate_up, ffn_w_down):
    batch, seq, d = x.shape
    depth = mix_pre_g.shape[0]
    mem_tokens = mem.shape[1]
    row = lambda g: g[None, :]

    kv = norm_matmul(mem.reshape(batch * mem_tokens, d), row(mem_norm_g), mem_w_kv.astype(BF16),
                     tm=batch * mem_tokens, tn=1024, out_dtype=BF16)

    h = x.reshape(batch * seq, d)
    for layer in range(depth):
        j = layer // 2
        gpre, gpost = row(mix_pre_g[layer]), row(mix_post_g[layer])
        if layer % 2 == 0:
            h = _even_mixer(h, gpre, gpost, ab_w_in[j], gla_gate_w2[j], gla_gate_b[j], gla_norm_g[j],
                            ab_w_out[j], batch=batch, seq=seq)
        else:
            h = _odd_mixer(h, gpre, gpost, gdn_w_in[j], gdn_conv_w[j], gdn_a_log[j], gdn_dt_bias[j],
                           gdn_norm_g[j], gdn_w_out[j], batch=batch, seq=seq)
        h = xattn_layer(h, row(xattn_pre_g[layer]), xattn_w_q[layer].astype(BF16), kv,
                        xattn_w_o[layer].astype(BF16), row(xattn_post_g[layer]), tm=512, seq=seq)
        h = ffn_layer(h, row(ffn_pre_g[layer]), ffn_w_gate_up[layer].astype(BF16),
                      ffn_w_down[layer].astype(BF16), row(ffn_post_g[layer]), tm=512, tf=512)
    return h.reshape(batch, seq, d)
```

```python
import functools

import jax
import jax.numpy as jnp
from jax import lax
from jax.experimental import pallas as pl
from jax.experimental.pallas import tpu as pltpu

F32 = jnp.float32
BF16 = jnp.bfloat16
RMS_EPS = 1e-6

SB_HEADS = 8
SB_DIM = 128
GLA_HEADS = 4
GLA_DK = 128
GLA_DV = 256
GLA_RANK = 16
GLA_NORMALIZER = 16.0
GDN_KH = 16
GDN_VH = 32
GDN_DIM = 128
GDN_CONV_K = 4
MEM_HEADS = 4
CHUNK = 64
LANES = 128
ROW_CHUNK = 64
SB_LOG_CUTOFF = -104.0


def _cparams(semantics, vmem_mib):
    return pltpu.CompilerParams(dimension_semantics=semantics, vmem_limit_bytes=vmem_mib << 20)


def _rms_rows(x, g):
    ms = jnp.mean(x * x, axis=-1, keepdims=True)
    return x * lax.rsqrt(ms + RMS_EPS) * g


def _softplus(z):
    return jnp.maximum(z, 0.0) + jnp.log1p(jnp.exp(-jnp.abs(z)))


def _silu(x):
    half = 0.5 * x
    return half + half * jnp.tanh(half)


def _split3(x):
    hi = x.astype(BF16)
    r1 = x - hi.astype(F32)
    mid = r1.astype(BF16)
    lo = (r1 - mid.astype(F32)).astype(BF16)
    return hi, mid, lo


def _for_rows(n_rows, body):
    def step(r, c):
        body(pl.ds(pl.multiple_of(r * ROW_CHUNK, ROW_CHUNK), ROW_CHUNK))
        return c
    lax.fori_loop(0, n_rows // ROW_CHUNK, step, 0, unroll=2)


def _prenorm_to(xn_ref, h_ref, g_ref):
    def body(rows):
        xn_ref[rows, :] = _rms_rows(h_ref[rows, :], g_ref[...]).astype(xn_ref.dtype)
    _for_rows(h_ref.shape[0], body)


def _postnorm_residual_to(o_ref, h_ref, acc_ref, g_ref):
    def body(rows):
        o_ref[rows, :] = h_ref[rows, :] + _rms_rows(acc_ref[rows, :], g_ref[...])
    _for_rows(h_ref.shape[0], body)


GATE_ROWS = 256


def _norm_matmul_kernel(h_ref, g_ref, w_ref, *rest, gate_rows):
    xn_ref = rest[-1]
    if gate_rows is None:
        (o_ref,) = rest[:-1]
    else:
        gate_in, o_ref, gate_o_ref = rest[:-3], rest[-3], rest[-2]

    @pl.when(pl.program_id(1) == 0)
    def _():
        _prenorm_to(xn_ref, h_ref, g_ref)
        if gate_rows is not None:
            for r in range(xn_ref.shape[0] // GATE_ROWS):
                rows = slice(r * GATE_ROWS, (r + 1) * GATE_ROWS)
                gate_o_ref[rows, :] = gate_rows(xn_ref[rows, :], *gate_in)

    o_ref[...] = jnp.dot(xn_ref[...], w_ref[...], preferred_element_type=F32).astype(o_ref.dtype)


def norm_matmul(h, g, w, *, tm, tn, out_dtype, gate=None):
    m, d = h.shape
    n = w.shape[1]
    gate_rows, gate_ops, gate_cols = gate if gate is not None else (None, (), 0)
    out_specs = [pl.BlockSpec((tm, tn), lambda i, j: (i, j))]
    out_shape = [jax.ShapeDtypeStruct((m, n), out_dtype)]
    if gate is not None:
        out_specs.append(pl.BlockSpec((tm, gate_cols), lambda i, j: (i, 0)))
        out_shape.append(jax.ShapeDtypeStruct((m, gate_cols), F32))
    outs = pl.pallas_call(
        functools.partial(_norm_matmul_kernel, gate_rows=gate_rows),
        grid=(m // tm, n // tn),
        in_specs=[pl.BlockSpec((tm, d), lambda i, j: (i, 0)),
                  pl.BlockSpec((1, d), lambda i, j: (0, 0)),
                  pl.BlockSpec((d, tn), lambda i, j: (0, j))]
                 + [pl.BlockSpec(a.shape, lambda i, j: (0, 0)) for a in gate_ops],
        out_specs=out_specs,
        out_shape=out_shape,
        scratch_shapes=[pltpu.VMEM((tm, d), BF16)],
        compiler_params=_cparams(("parallel", "arbitrary"), 48),
        name="norm_matmul",
    )(h, g, w, *gate_ops)
    return outs if gate is not None else outs[0]


def _matmul_post_kernel(*refs):
    a_refs, (w_ref, g_ref, h_ref, o_ref, acc_ref) = refs[:-5], refs[-5:]
    k0 = 0
    for idx, a_ref in enumerate(a_refs):
        kw = a_ref.shape[1]
        part = jnp.dot(a_ref[...], w_ref[k0:k0 + kw, :], preferred_element_type=F32)
        if idx == 0:
            acc_ref[...] = part
        else:
            acc_ref[...] += part
        k0 += kw
    _postnorm_residual_to(o_ref, h_ref, acc_ref, g_ref)


def matmul_post(a_slabs, w, g, h, *, tm):
    m = h.shape[0]
    k, d = w.shape
    return pl.pallas_call(
        _matmul_post_kernel,
        grid=(m // tm,),
        in_specs=[pl.BlockSpec((tm, a.shape[1]), lambda i: (i, 0)) for a in a_slabs] + [
                  pl.BlockSpec((k, d), lambda i: (0, 0), pipeline_mode=pl.Buffered(1)),
                  pl.BlockSpec((1, d), lambda i: (0, 0)),
                  pl.BlockSpec((tm, d), lambda i: (i, 0))],
        out_specs=pl.BlockSpec((tm, d), lambda i: (i, 0)),
        out_shape=jax.ShapeDtypeStruct((m, d), F32),
        scratch_shapes=[pltpu.VMEM((tm, d), F32)],
        compiler_params=_cparams(("parallel",), ((2 * tm * k * 2 + k * d * 2 + 5 * tm * d * 4) >> 20) + 8),
        name="matmul_post",
    )(*a_slabs, w, g, h)


def _ffn_kernel(h_ref, gpre_ref, wg_ref, wu_ref, wd_ref, gpost_ref, o_ref, xn_ref, acc_ref):
    j = pl.program_id(1)

    @pl.when(j == 0)
    def _():
        _prenorm_to(xn_ref, h_ref, gpre_ref)
        acc_ref[...] = jnp.zeros_like(acc_ref)

    xn = xn_ref[...]
    gate = jnp.dot(xn, wg_ref[...], preferred_element_type=F32)
    up = jnp.dot(xn, wu_ref[...], preferred_element_type=F32)
    act = (_silu(gate) * up).astype(BF16)
    acc_ref[...] += jnp.dot(act, wd_ref[...], preferred_element_type=F32)

    @pl.when(j == pl.num_programs(1) - 1)
    def _():
        _postnorm_residual_to(o_ref, h_ref, acc_ref, gpost_ref)


def ffn_layer(h, gpre, w_gate_up, w_down, gpost, *, tm, tf):
    m, d = h.shape
    f = w_down.shape[0]
    nf = f // tf
    return pl.pallas_call(
        _ffn_kernel,
        grid=(m // tm, nf),
        in_specs=[pl.BlockSpec((tm, d), lambda i, j: (i, 0)),
                  pl.BlockSpec((1, d), lambda i, j: (0, 0)),
                  pl.BlockSpec((d, tf), lambda i, j: (0, j)),
                  pl.BlockSpec((d, tf), lambda i, j: (0, j + nf)),
                  pl.BlockSpec((tf, d), lambda i, j: (j, 0)),
                  pl.BlockSpec((1, d), lambda i, j: (0, 0))],
        out_specs=pl.BlockSpec((tm, d), lambda i, j: (i, 0)),
        out_shape=jax.ShapeDtypeStruct((m, d), F32),
        scratch_shapes=[pltpu.VMEM((tm, d), BF16), pltpu.VMEM((tm, d), F32)],
        compiler_params=_cparams(("parallel", "arbitrary"), 48),
        name="ffn_layer",
    )(h, gpre, w_gate_up, w_gate_up, w_down, gpost)


def _xattn_kernel(h_ref, gpre_ref, wq_ref, k_ref, v_ref, wo_ref, gpost_ref, o_ref, xn_ref, att_ref, acc_ref):
    _prenorm_to(xn_ref, h_ref, gpre_ref)
    d = h_ref.shape[1]
    hd = d // MEM_HEADS
    xn = xn_ref[...]
    for i in range(MEM_HEADS):
        cols = slice(i * hd, (i + 1) * hd)
        q = jnp.dot(xn, wq_ref[:, cols], preferred_element_type=F32)
        s = lax.dot_general(q.astype(BF16), k_ref[:, cols], (((1,), (1,)), ((), ())),
                            preferred_element_type=F32) * (hd ** -0.5)
        e = jnp.exp(s - jnp.max(s, axis=-1, keepdims=True))
        p = e / jnp.sum(e, axis=-1, keepdims=True)
        att_ref[:, cols] = jnp.dot(p.astype(BF16), v_ref[:, cols], preferred_element_type=F32).astype(BF16)
    acc_ref[...] = jnp.dot(att_ref[...], wo_ref[...], preferred_element_type=F32)
    _postnorm_residual_to(o_ref, h_ref, acc_ref, gpost_ref)


def xattn_layer(h, gpre, wq, kv, wo, gpost, *, tm, seq):
    m, d = h.shape
    mt = kv.shape[0] // (m // seq)
    per_b = seq // tm
    once = pl.Buffered(1)
    vmem = 4 * tm * d * 4 + 2 * d * d * 2 + 4 * mt * d * 2 + 2 * tm * d * 2 + tm * d * 4
    return pl.pallas_call(
        _xattn_kernel,
        grid=(m // tm,),
        in_specs=[pl.BlockSpec((tm, d), lambda i: (i, 0)),
                  pl.BlockSpec((1, d), lambda i: (0, 0)),
                  pl.BlockSpec((d, d), lambda i: (0, 0), pipeline_mode=once),
                  pl.BlockSpec((mt, d), lambda i: (i // per_b, 0)),
                  pl.BlockSpec((mt, d), lambda i: (i // per_b, 1)),
                  pl.BlockSpec((d, d), lambda i: (0, 0), pipeline_mode=once),
                  pl.BlockSpec((1, d), lambda i: (0, 0))],
        out_specs=pl.BlockSpec((tm, d), lambda i: (i, 0)),
        out_shape=jax.ShapeDtypeStruct((m, d), F32),
        scratch_shapes=[pltpu.VMEM((tm, d), BF16), pltpu.VMEM((tm, d), BF16), pltpu.VMEM((tm, d), F32)],
        compiler_params=_cparams(("parallel",), (vmem >> 20) + 10),
        name="xattn_layer",
    )(h, gpre, wq, kv, kv, wo, gpost)


def _split_heads(x, n, width):
    return jnp.stack([x[:, i * width:(i + 1) * width] for i in range(n)], axis=0)


def _bmm(a, b):
    return lax.dot_general(a, b, (((2,), (1,)), ((0,), (0,))), preferred_element_type=F32)


def _bmm_nt(a, b):
    return lax.dot_general(a, b, (((2,), (2,)), ((0,), (0,))), preferred_element_type=F32)


def _sb_kernel(q_ref, k_ref, v_ref, o_ref, carry_ref, acc_ref):
    tq = q_ref.shape[0]
    nh = SB_HEADS
    qi = pl.program_id(1)
    scale = SB_DIM ** -0.5
    q = _split_heads(q_ref[...], nh, SB_DIM)
    row = lax.broadcasted_iota(jnp.int32, (tq, tq), 0)
    col = lax.broadcasted_iota(jnp.int32, (tq, tq), 1)
    after = jnp.where(row > col, 1.0, 0.0).astype(BF16)

    def block(j, diagonal):
        ks = pl.ds(pl.multiple_of(j * tq, tq), tq)
        kb = _split_heads(k_ref[ks, :], nh, SB_DIM)
        vb = _split_heads(v_ref[ks, :], nh, SB_DIM)
        z = _bmm_nt(q, kb) * scale
        sp = _softplus(z)
        log_keep = -sp
        if diagonal:
            valid = col < row
            log_keep = jnp.where(valid, log_keep, 0.0)
        hi, mid, lo = _split3(log_keep.reshape(nh * tq, tq))
        later = (jnp.dot(hi, after, preferred_element_type=F32)
                 + jnp.dot(mid, after, preferred_element_type=F32)
                 + jnp.dot(lo, after, preferred_element_type=F32)).reshape(nh, tq, tq)
        w = jnp.exp(z - sp + later + carry_ref[...])
        if diagonal:
            w = jnp.where(valid, w, 0.0)
        acc_ref[...] += _bmm(w.astype(BF16), vb)
        carry_ref[...] += jnp.sum(log_keep, axis=-1, keepdims=True)
        return jnp.max(carry_ref[...])

    carry_ref[...] = jnp.zeros_like(carry_ref)
    acc_ref[...] = jnp.zeros_like(acc_ref)
    top = block(qi, True)

    def cond(state):
        j, top = state
        return jnp.logical_and(j >= 0, top > SB_LOG_CUTOFF)

    def body(state):
        j, _ = state
        return j - 1, block(j, False)

    lax.while_loop(cond, body, (qi - 1, top))
    for i in range(nh):
        o_ref[:, i * SB_DIM:(i + 1) * SB_DIM] = acc_ref[i].astype(o_ref.dtype)


def sb_attention(qkv, *, batch, seq, tq):
    m = qkv.shape[0]
    nq = seq // tq
    width = SB_HEADS * SB_DIM
    return pl.pallas_call(
        _sb_kernel,
        grid=(batch, nq),
        in_specs=[pl.BlockSpec((tq, width), lambda b, i: (b * nq + i, 0)),
                  pl.BlockSpec((seq, width), lambda b, i: (b, 1)),
                  pl.BlockSpec((seq, width), lambda b, i: (b, 2))],
        out_specs=pl.BlockSpec((tq, width), lambda b, i: (b * nq + i, 0)),
        out_shape=jax.ShapeDtypeStruct((m, width), BF16),
        scratch_shapes=[pltpu.VMEM((SB_HEADS, tq, 1), F32), pltpu.VMEM((SB_HEADS, tq, SB_DIM), F32)],
        compiler_params=_cparams(("parallel", "arbitrary"), 32),
        name="sb_attention",
    )(qkv, qkv, qkv)


def _chunk_cumsum_matrix(n):
    row = lax.broadcasted_iota(jnp.int32, (n, n), 0)
    col = lax.broadcasted_iota(jnp.int32, (n, n), 1)
    same = (row // CHUNK) == (col // CHUNK)
    return jnp.where(jnp.logical_and(same, col <= row), 1.0, 0.0).astype(BF16)


def _chunk_cumsum(x):
    tri = _chunk_cumsum_matrix(x.shape[0])
    hi, mid, lo = _split3(x)
    return (jnp.dot(tri, hi, preferred_element_type=F32)
            + jnp.dot(tri, mid, preferred_element_type=F32)
            + jnp.dot(tri, lo, preferred_element_type=F32))


def _gla_gate_rows(xn, w1_ref, w2_ref, b_ref):
    low = jnp.dot(xn, w1_ref[...], preferred_element_type=F32)
    pre = jnp.dot(low.astype(BF16), w2_ref[...], preferred_element_type=F32) + b_ref[...]
    log_a = -_softplus(-pre) * (1.0 / GLA_NORMALIZER)
    return _chunk_cumsum(log_a)


def _gla_kernel(q_ref, k_ref, v_ref, r_ref, gc_ref, ng_ref, o_ref, st_ref):
    @pl.when(pl.program_id(1) == 0)
    def _():
        st_ref[...] = jnp.zeros_like(st_ref)

    tb = q_ref.shape[0]
    nh = GLA_HEADS
    row = lax.broadcasted_iota(jnp.int32, (CHUNK, CHUNK), 0)
    col = lax.broadcasted_iota(jnp.int32, (CHUNK, CHUNK), 1)
    causal = col <= row
    tn = (((1,), (1,)), ((0,), (0,)))
    for c in range(tb // CHUNK):
        rows = slice(c * CHUNK, (c + 1) * CHUNK)
        g = _split_heads(gc_ref[rows, :], nh, GLA_DK)
        g_last = g[:, CHUNK - 1:CHUNK, :]
        k = _split_heads(k_ref[rows, :], nh, GLA_DK)
        v = _split_heads(v_ref[rows, :], nh, GLA_DV).astype(BF16)
        q_dec = (_split_heads(q_ref[rows, :], nh, GLA_DK) * (GLA_DK ** -0.5) * jnp.exp(g)).astype(BF16)
        k_inv = (k * jnp.exp(-g)).astype(BF16)
        k_dec = (k * jnp.exp(g_last - g)).astype(BF16)
        scores = jnp.where(causal, _bmm_nt(q_dec, k_inv), 0.0).astype(BF16)
        st = st_ref[...]
        o = _bmm(scores, v) + _bmm_nt(q_dec, st.astype(BF16))
        st_ref[...] = st * jnp.exp(g_last) + lax.dot_general(v, k_dec, tn, preferred_element_type=F32)
        y = _rms_rows(o, ng_ref[...]) * _silu(_split_heads(r_ref[rows, :], nh, GLA_DV))
        for i in range(nh):
            o_ref[rows, i * GLA_DV:(i + 1) * GLA_DV] = y[i].astype(o_ref.dtype)


def gla_attention(proj, gc, norm_g, *, batch, seq, tb):
    m = proj.shape[0]
    nt = seq // tb
    kw, vw = GLA_HEADS * GLA_DK, GLA_HEADS * GLA_DV
    return pl.pallas_call(
        _gla_kernel,
        grid=(batch, nt),
        in_specs=[pl.BlockSpec((tb, kw), lambda b, t: (b * nt + t, 0)),
                  pl.BlockSpec((tb, kw), lambda b, t: (b * nt + t, 1)),
                  pl.BlockSpec((tb, vw), lambda b, t: (b * nt + t, 2 * kw // vw)),
                  pl.BlockSpec((tb, vw), lambda b, t: (b * nt + t, 2 * kw // vw + 1)),
                  pl.BlockSpec((tb, kw), lambda b, t: (b * nt + t, 0)),
                  pl.BlockSpec((1, GLA_DV), lambda b, t: (0, 0))],
        out_specs=pl.BlockSpec((tb, vw), lambda b, t: (b * nt + t, 0)),
        out_shape=jax.ShapeDtypeStruct((m, vw), BF16),
        scratch_shapes=[pltpu.VMEM((GLA_HEADS, GLA_DV, GLA_DK), F32)],
        compiler_params=_cparams(("parallel", "arbitrary"), 32),
        name="gla_attention",
    )(proj, proj, proj, proj, gc, norm_g)


def _gdn_gate_rows(xn, w_ref, alog_ref, dtb_ref, *, n_heads):
    ba = jnp.dot(xn, w_ref[...], preferred_element_type=F32)
    beta = jax.nn.sigmoid(ba)
    g = -jnp.exp(alog_ref[...]) * _softplus(ba + dtb_ref[...])
    lane = lax.broadcasted_iota(jnp.int32, ba.shape, 1)
    return jnp.where((lane & (LANES - 1)) < n_heads, beta, _chunk_cumsum(g))


CONV_HALO = 8


def _conv_silu_to(o_ref, x_ref, prev_ref, w_ref, xs_ref, first, l2_normalize):
    tb, tc = x_ref.shape
    halo = CONV_HALO
    xs_ref[0:halo, :] = jnp.where(first, 0.0, prev_ref[...])
    xs_ref[halo:halo + tb, :] = x_ref[...]
    for r in range(tb // ROW_CHUNK):
        for hh in range(tc // GDN_DIM):
            cols = slice(hh * GDN_DIM, (hh + 1) * GDN_DIM)
            y = jnp.zeros((ROW_CHUNK, GDN_DIM), F32)
            for i in range(GDN_CONV_K):
                off = r * ROW_CHUNK + halo - (GDN_CONV_K - 1) + i
                y = y + w_ref[i:i + 1, cols] * xs_ref[off:off + ROW_CHUNK, cols]
            y = _silu(y)
            if l2_normalize:
                y = y * lax.rsqrt(jnp.sum(y * y, axis=-1, keepdims=True) + RMS_EPS)
            o_ref[r * ROW_CHUNK:(r + 1) * ROW_CHUNK, cols] = y.astype(o_ref.dtype)


def _gdn_kernel(qx_ref, kx_ref, vx_ref, qp_ref, kp_ref, vp_ref, qw_ref, kw_ref, vw_ref,
                z_ref, col_ref, grow_ref, glast_ref, ng_ref, o_ref,
                s_ref, u_ref, w_ref, qkd_ref, q16_ref, k16_ref, eg_ref, ed_ref, egl_ref,
                q_ref, k_ref, v_ref, qs_ref, ks_ref, vs_ref):
    first = pl.program_id(2) == 0

    @pl.when(first)
    def _():
        s_ref[...] = jnp.zeros_like(s_ref)

    _conv_silu_to(q_ref, qx_ref, qp_ref, qw_ref, qs_ref, first, True)
    _conv_silu_to(k_ref, kx_ref, kp_ref, kw_ref, ks_ref, first, True)
    _conv_silu_to(v_ref, vx_ref, vp_ref, vw_ref, vs_ref, first, False)

    groups, nc = u_ref.shape[0], u_ref.shape[1]
    n = groups * nc
    d = GDN_DIM
    c2 = 2 * CHUNK
    row = lax.broadcasted_iota(jnp.int32, (CHUNK, c2), 0)
    lane = lax.broadcasted_iota(jnp.int32, (CHUNK, c2), 1)
    first_head = lane < CHUNK
    key_pos = jnp.where(first_head, lane, lane - CHUNK)
    incl = key_pos <= row
    strict = key_pos < row
    diag = key_pos == row

    def chunked(ref, width):
        return jnp.concatenate([ref[:, g * width:(g + 1) * width].reshape(nc, CHUNK, width)
                                for g in range(groups)], axis=0)

    def block_diag(m16):
        zero = jnp.zeros_like(m16)
        return jnp.concatenate([jnp.where(first_head, m16, zero), jnp.where(first_head, zero, m16)], axis=1)

    q = chunked(q_ref, d).astype(F32) * (d ** -0.5)
    k = chunked(k_ref, d).astype(F32)
    v = chunked(v_ref, 2 * d).astype(F32)
    cols = col_ref[...].reshape(nc, CHUNK, LANES)
    nv = 2 * groups

    def token_column(lane0):
        return jnp.concatenate([cols[:, :, lane0 + 2 * g:lane0 + 2 * g + 1] for g in range(groups)], axis=0)

    beta_a, beta_b = token_column(0), token_column(1)
    g_a, g_b = token_column(nv), token_column(nv + 1)
    g_row = jnp.concatenate([grow_ref[:, g:g + 1, :] for g in range(groups)], axis=0)
    g_end = jnp.concatenate([glast_ref[:, g:g + 1, :] for g in range(groups)], axis=0)

    k16 = k.astype(BF16)
    q16 = q.astype(BF16)
    prod = _bmm_nt(jnp.concatenate([k16, q16], axis=1), jnp.concatenate([k16, k16], axis=1))
    kk2, qk2 = prod[:, :CHUNK], prod[:, CHUNK:]

    g_col2 = jnp.where(first_head, g_a, g_b)
    beta2 = jnp.where(first_head, beta_a, beta_b)
    decay2 = jnp.where(incl, jnp.exp(jnp.where(incl, g_col2 - g_row, 0.0)), 0.0)
    x = jnp.where(strict, -(beta2 * kk2 * decay2), 0.0)
    qkd_ref[...] = (qk2 * decay2).astype(BF16).reshape(qkd_ref.shape)

    r = jnp.zeros_like(x)
    p = jnp.where((row >> 3) == (key_pos >> 3), x, 0.0)
    for _ in range(3):
        p16 = p.astype(BF16)
        y = _bmm(jnp.concatenate([r.astype(BF16), p16], axis=1), block_diag(p16))
        r = r + p + y[:, :CHUNK]
        p = y[:, CHUNK:]
    tinv = r + jnp.where(diag, 1.0, 0.0)
    for shift in (3, 4, 5):
        couples = jnp.logical_and((row >> (shift + 1)) == (key_pos >> (shift + 1)),
                                  (row >> shift) != (key_pos >> shift))
        xc16 = jnp.where(couples, x, 0.0).astype(BF16)
        t16 = tinv.astype(BF16)
        y = _bmm(xc16, block_diag(t16))
        tinv = tinv + _bmm(t16, block_diag(y.astype(BF16)))
    r = jnp.where(diag, 0.0, tinv)

    eg_a, eg_b = jnp.exp(g_a), jnp.exp(g_b)
    vb_a, vb_b = v[:, :, :d] * beta_a, v[:, :, d:] * beta_b
    kg_a, kg_b = k * (beta_a * eg_a), k * (beta_b * eg_b)
    rhs2 = jnp.concatenate([jnp.concatenate([vb_a, kg_a], axis=2),
                            jnp.concatenate([vb_b, kg_b], axis=2)], axis=1).astype(BF16)
    corr = _bmm(block_diag(r.astype(BF16)), rhs2)
    u_ref[...] = jnp.concatenate([vb_a + corr[:, :CHUNK, :d], vb_b + corr[:, CHUNK:, :d]],
                                 axis=2).reshape(u_ref.shape)
    w_ref[...] = jnp.concatenate([kg_a + corr[:, :CHUNK, d:], kg_b + corr[:, CHUNK:, d:]],
                                 axis=1).astype(BF16).reshape(w_ref.shape)
    q16_ref[...] = q16.reshape(q16_ref.shape)
    k16_ref[...] = k16.reshape(k16_ref.shape)
    lanes_d = (n, CHUNK, d)
    eg_ref[...] = jnp.concatenate([jnp.broadcast_to(eg_a, lanes_d), jnp.broadcast_to(eg_b, lanes_d)],
                                  axis=2).reshape(eg_ref.shape)
    ed_ref[...] = jnp.concatenate([jnp.exp(g_end[:, :, :d] - g_a), jnp.exp(g_end[:, :, d:] - g_b)],
                                  axis=2).reshape(ed_ref.shape)
    egl_ref[...] = jnp.exp(g_end).reshape(egl_ref.shape)

    zeros_vd = jnp.zeros((groups, CHUNK, d), BF16)
    tn = (((1,), (1,)), ((0,), (0,)))
    for c in range(nc):
        rows = slice(c * CHUNK, (c + 1) * CHUNK)
        s = s_ref[...]
        s16 = s.astype(BF16)
        ws = _bmm(jnp.concatenate([w_ref[:, c], q16_ref[:, c]], axis=1), s16)
        vn = u_ref[:, c] - jnp.concatenate([ws[:, :CHUNK, :d], ws[:, CHUNK:c2, d:]], axis=2)
        vn16 = vn.astype(BF16)
        vn_bd = jnp.concatenate([jnp.concatenate([vn16[:, :, :d], zeros_vd], axis=2),
                                 jnp.concatenate([zeros_vd, vn16[:, :, d:]], axis=2)], axis=1)
        o = eg_ref[:, c] * ws[:, c2:] + _bmm(qkd_ref[:, c], vn_bd)
        vn_dec = (vn * ed_ref[:, c]).astype(BF16)
        s_ref[...] = s * egl_ref[:, c] + lax.dot_general(k16_ref[:, c], vn_dec, tn,
                                                         preferred_element_type=F32)
        for g in range(groups):
            for e in range(2):
                cs = slice((2 * g + e) * d, (2 * g + e + 1) * d)
                o_ref[rows, cs] = (_rms_rows(o[g, :, e * d:(e + 1) * d], ng_ref[...])
                                   * _silu(z_ref[rows, cs])).astype(o_ref.dtype)


def gdn_attention(proj, conv_w, colv, grow, glast, norm_g, *, batch, seq, tb, groups):
    m = proj.shape[0]
    nt = seq // tb
    nc = tb // CHUNK
    d = GDN_DIM
    kw, vw = groups * d, groups * 2 * d
    k0 = GDN_KH * d // kw
    v0 = 2 * GDN_KH * d // vw
    z0 = (2 * GDN_KH + GDN_VH) * d // vw
    per = lambda *shape: (groups, nc) + shape
    halo_blocks = tb // CONV_HALO
    prev_rows = lambda b, t: jnp.maximum((b * nt + t) * halo_blocks - 1, 0)
    return pl.pallas_call(
        _gdn_kernel,
        grid=(batch, GDN_KH // groups, nt),
        in_specs=[pl.BlockSpec((tb, kw), lambda b, h, t: (b * nt + t, h)),
                  pl.BlockSpec((tb, kw), lambda b, h, t: (b * nt + t, k0 + h)),
                  pl.BlockSpec((tb, vw), lambda b, h, t: (b * nt + t, v0 + h)),
                  pl.BlockSpec((CONV_HALO, kw), lambda b, h, t: (prev_rows(b, t), h)),
                  pl.BlockSpec((CONV_HALO, kw), lambda b, h, t: (prev_rows(b, t), k0 + h)),
                  pl.BlockSpec((CONV_HALO, vw), lambda b, h, t: (prev_rows(b, t), v0 + h)),
                  pl.BlockSpec((GDN_CONV_K, kw), lambda b, h, t: (0, h)),
                  pl.BlockSpec((GDN_CONV_K, kw), lambda b, h, t: (0, k0 + h)),
                  pl.BlockSpec((GDN_CONV_K, vw), lambda b, h, t: (0, v0 + h)),
                  pl.BlockSpec((tb, vw), lambda b, h, t: (b * nt + t, z0 + h)),
                  pl.BlockSpec((tb, LANES), lambda b, h, t: (b * nt + t, h)),
                  pl.BlockSpec((None, None, nc, 8, 2 * CHUNK), lambda b, h, t: (b, h, t, 0, 0)),
                  pl.BlockSpec((None, None, nc, 8, 2 * d), lambda b, h, t: (b, h, t, 0, 0)),
                  pl.BlockSpec((1, d), lambda b, h, t: (0, 0))],
        out_specs=pl.BlockSpec((tb, vw), lambda b, h, t: (b * nt + t, h)),
        out_shape=jax.ShapeDtypeStruct((m, GDN_VH * d), BF16),
        scratch_shapes=[pltpu.VMEM((groups, d, 2 * d), F32),
                        pltpu.VMEM(per(CHUNK, 2 * d), F32),
                        pltpu.VMEM(per(2 * CHUNK, d), BF16),
                        pltpu.VMEM(per(CHUNK, 2 * CHUNK), BF16),
                        pltpu.VMEM(per(CHUNK, d), BF16),
                        pltpu.VMEM(per(CHUNK, d), BF16),
                        pltpu.VMEM(per(CHUNK, 2 * d), F32),
                        pltpu.VMEM(per(CHUNK, 2 * d), F32),
                        pltpu.VMEM(per(1, 2 * d), F32),
                        pltpu.VMEM((tb, kw), F32),
                        pltpu.VMEM((tb, kw), F32),
                        pltpu.VMEM((tb, vw), F32),
                        pltpu.VMEM((tb + CONV_HALO, kw), F32),
                        pltpu.VMEM((tb + CONV_HALO, kw), F32),
                        pltpu.VMEM((tb + CONV_HALO, vw), F32)],
        compiler_params=_cparams(("parallel", "parallel", "arbitrary"), 56),
        name="gdn_attention",
    )(proj, proj, proj, proj, proj, proj, conv_w, conv_w, conv_w, proj, colv, grow, glast, norm_g)


def _pad_cols(w, n):
    return jnp.pad(w, ((0, 0), (0, n - w.shape[1])))


def _even_mixer(h, gpre, gpost, w_in, gate_w2, gate_b, gla_norm_g, w_out, *, batch, seq):
    n_sb = 3 * SB_HEADS * SB_DIM
    n_main = w_in.shape[1] - GLA_RANK
    sb_qkv = norm_matmul(h, gpre, w_in[:, :n_sb].astype(BF16), tm=1024, tn=1024, out_dtype=BF16)
    w1 = _pad_cols(w_in[:, n_main:], LANES).astype(BF16)
    w2 = jnp.pad(gate_w2, ((0, LANES - GLA_RANK), (0, 0))).astype(BF16)
    gla_proj, gc = norm_matmul(h, gpre, w_in[:, n_sb:n_main].astype(BF16), tm=1024, tn=1024, out_dtype=F32,
                               gate=(_gla_gate_rows, (w1, w2, gate_b[None, :]), gate_w2.shape[1]))
    o_sb = sb_attention(sb_qkv, batch=batch, seq=seq, tq=128)
    o_gla = gla_attention(gla_proj, gc, gla_norm_g[None, :], batch=batch, seq=seq, tb=256)
    return matmul_post((o_sb, o_gla), w_out.astype(BF16), gpost, h, tm=512)


def _odd_mixer(h, gpre, gpost, w_in, conv_w, a_log, dt_bias, norm_g, w_out, *, batch, seq):
    n_main = w_in.shape[1] - 2 * GDN_VH
    groups = 8
    ng, nv = GDN_KH // groups, 2 * groups
    d_model = w_in.shape[0]
    slab = lambda first, second, lead: jnp.concatenate(
        [first.reshape(lead + (ng, nv)), second.reshape(lead + (ng, nv)),
         jnp.zeros(lead + (ng, LANES - 2 * nv), F32)], axis=-1).reshape(lead + (ng * LANES,))
    w_ba = slab(w_in[:, n_main:n_main + GDN_VH], w_in[:, n_main + GDN_VH:], (d_model,)).astype(BF16)
    zeros_vh = jnp.zeros((GDN_VH,), F32)
    gate_ops = (w_ba, slab(zeros_vh, a_log, ())[None, :], slab(zeros_vh, dt_bias, ())[None, :])
    proj, gates = norm_matmul(h, gpre, w_in[:, :n_main].astype(BF16), tm=1024, tn=1024, out_dtype=F32,
                              gate=(functools.partial(_gdn_gate_rows, n_heads=nv), gate_ops, ng * LANES))

    nc = seq // CHUNK
    gcum = gates.reshape(batch, nc, CHUNK, ng, LANES)[..., nv:2 * nv].reshape(batch, nc, CHUNK, ng, groups, 2)
    pad_sublanes = lambda a: jnp.pad(a, ((0, 0), (0, 0), (0, 0), (0, 8 - groups), (0, 0)))
    grow = pad_sublanes(gcum.transpose(0, 3, 1, 4, 5, 2).reshape(batch, ng, nc, groups, 2 * CHUNK))
    glast = pad_sublanes(jnp.repeat(gcum[:, :, CHUNK - 1].transpose(0, 2, 1, 3, 4), GDN_DIM, axis=-1))

    o = gdn_attention(proj, conv_w, gates, grow, glast, norm_g[None, :], batch=batch, seq=seq, tb=256,
                      groups=groups)
    return matmul_post((o,), w_out.astype(BF16), gpost, h, tm=512)


def kernel(x, mem, mem_norm_g, mem_w_kv, mix_pre_g, mix_post_g, ab_w_in, gla_gate_w2, gla_gate_b, gla_norm_g, ab_w_out, gdn_w_in, gdn_conv_w, gdn_a_log, gdn_dt_bias, gdn_norm_g, gdn_w_out, xattn_pre_g, xattn_post_g, xattn_w_q, xattn_w_o, ffn_pre_g, ffn_post_g, ffn_w_gate_up, ffn_w_down):
    batch, seq, d = x.shape
    depth = mix_pre_g.shape[0]
    mem_tokens = mem.shape[1]
    row = lambda g: g[None, :]

    kv = norm_matmul(mem.reshape(batch * mem_tokens, d), row(mem_norm_g), mem_w_kv.astype(BF16),
                     tm=batch * mem_tokens, tn=1024, out_dtype=BF16)

    h = x.reshape(batch * seq, d)
    for layer in range(depth):
        j = layer // 2
        gpre, gpost = row(mix_pre_g[layer]), row(mix_post_g[layer])
        if layer % 2 == 0:
            h = _even_mixer(h, gpre, gpost, ab_w_in[j], gla_gate_w2[j], gla_gate_b[j], gla_norm_g[j],
                            ab_w_out[j], batch=batch, seq=seq)
        else:
            h = _odd_mixer(h, gpre, gpost, gdn_w_in[j], gdn_conv_w[j], gdn_a_log[j], gdn_dt_bias[j],
                           gdn_norm_g[j], gdn_w_out[j], batch=batch, seq=seq)
        h = xattn_layer(h, row(xattn_pre_g[layer]), xattn_w_q[layer].astype(BF16), kv,
                        xattn_w_o[layer].astype(BF16), row(xattn_post_g[layer]), tm=512, seq=seq)
        h = ffn_layer(h, row(ffn_pre_g[layer]), ffn_w_gate_up[layer].astype(BF16),
                      ffn_w_down[layer].astype(BF16), row(ffn_post_g[layer]), tm=512, tf=512)
    return h.reshape(batch, seq, d)
```
